```python
import functools
import jax, jax.numpy as jnp
from jax import lax
import numpy as np

D_MODEL = 2048
BATCH = 1
SEQ = 8192
DEPTH = 1
DEC_BATCH = 128
DEC_SEQ = 1
PAST_LEN = 2048
PAGE_SIZE = 128

ML_HEADS = 4
ML_DQK = 256
ML_DV = 512
ML_CHUNK = 128
ML_W_QK = ML_HEADS * ML_DQK
ML_W_V = ML_HEADS * ML_DV
FX_HEADS = 16
FX_HD = 128
FX_W = FX_HEADS * FX_HD
FX_QBLOCK = 128
FX_SCALE = FX_HD ** -0.5
N_IN = 2 * ML_W_QK + 2 * ML_W_V + 2 * ML_HEADS + 3 * FX_W + FX_HEADS + 2 * D_MODEL
N_EXPERTS = 256
TOP_K = 8
N_GROUPS = 8
TOPK_GROUPS = 4
D_EXPERT = 512
D_SHARED = 512
ROUTED_SCALE = 2.5
MOE_BLOCK = 128
MOE_SMALL_BLOCK = 8
PLE_DIM = 256
LN_EPS = 1e-5
DEEPNORM_ALPHA = (2 * DEPTH) ** 0.25

kernel_name = 'hybrid_mlstm_fox_moe_step'


def _split_points():
    sizes = (ML_W_QK, ML_W_QK, ML_W_V, ML_W_V, ML_HEADS, ML_HEADS, FX_W, FX_W, FX_W, FX_HEADS, D_MODEL, D_MODEL)
    pts, acc = [], 0
    for s in sizes[:-1]:
        acc += s
        pts.append(acc)
    return pts


def layer_norm(x, g, b):
    xf = x.astype(jnp.float32)
    mu = jnp.mean(xf, -1, keepdims=True)
    var = jnp.mean(jnp.square(xf - mu), -1, keepdims=True)
    return ((xf - mu) * lax.rsqrt(var + LN_EPS)).astype(x.dtype) * g + b


def swiglu(h, wg, wu, wd):
    return (jax.nn.silu(h @ wg) * (h @ wu)) @ wd


def mlstm_chunk(state, inp):
    C, n, m = state
    q, k, v, li, lf = inp
    L = q.shape[2]
    b = jnp.cumsum(lf, axis=-1)
    causal = jnp.tril(jnp.ones((L, L), bool))
    dlog = jnp.where(causal, b[..., :, None] - b[..., None, :] + li[..., None, :], -jnp.inf)
    inter = b + m[..., None]
    m_row = jnp.maximum(inter, jnp.max(dlog, -1))
    w_inter = jnp.exp(inter - m_row)
    qk = jnp.einsum('bhtd,bhsd->bhts', q, k) * jnp.exp(dlog - m_row[..., None])
    num = w_inter[..., None] * jnp.einsum('bhtd,bhde->bhte', q, C) + jnp.einsum('bhts,bhse->bhte', qk, v)
    den = w_inter * jnp.einsum('bhtd,bhd->bht', q, n) + jnp.sum(qk, -1)
    h = num / jnp.maximum(jnp.abs(den), jnp.exp(-m_row))[..., None]
    b_tot = b[..., -1]
    lw = b_tot[..., None] - b + li
    m_new = jnp.maximum(b_tot + m, jnp.max(lw, -1))
    decay = jnp.exp(b_tot + m - m_new)
    wk = k * jnp.exp(lw - m_new[..., None])[..., None]
    C_new = decay[..., None, None] * C + jnp.einsum('bhsd,bhse->bhde', wk, v)
    n_new = decay[..., None] * n + jnp.sum(wk, axis=2)
    return (C_new, n_new, m_new), h


def mlstm(q, k, v, li, lf, C0, n0, m0):
    Bsz, T = q.shape[:2]
    L = ML_CHUNK if T % ML_CHUNK == 0 else T
    nc = T // L

    def to_chunks(a):
        a = a.astype(jnp.float32).reshape((Bsz, nc, L) + a.shape[2:])
        return jnp.swapaxes(jnp.moveaxis(a, 1, 0), 2, 3)

    xs = tuple(to_chunks(a) for a in (q, k, v, li, lf))
    init = (C0.astype(jnp.float32), n0.astype(jnp.float32), m0.astype(jnp.float32))
    state, h = lax.scan(mlstm_chunk, init, xs)
    h = jnp.transpose(h, (1, 0, 3, 2, 4)).reshape(Bsz, T, ML_HEADS, ML_DV)
    return h, state


def fox_prompt(q, k, v, logf):
    Bsz, T = q.shape[:2]
    nb = T // FX_QBLOCK
    c = jnp.transpose(jnp.cumsum(logf.astype(jnp.float32), axis=1), (0, 2, 1))
    qb = jnp.moveaxis(q.reshape(Bsz, nb, FX_QBLOCK, FX_HEADS, FX_HD), 1, 0)
    cb = jnp.moveaxis(c.reshape(Bsz, FX_HEADS, nb, FX_QBLOCK), 2, 0)
    k_pos = jnp.arange(T)

    def block(args):
        j, qj, cj = args
        s = jnp.einsum('bqhd,bkhd->bhqk', qj, k, preferred_element_type=jnp.float32) * FX_SCALE
        s = s + (cj[..., :, None] - c[..., None, :])
        q_pos = j * FX_QBLOCK + jnp.arange(FX_QBLOCK)
        s = jnp.where(k_pos[None, :] <= q_pos[:, None], s, -jnp.inf)
        pr = jax.nn.softmax(s, axis=-1).astype(v.dtype)
        return jnp.einsum('bhqk,bkhd->bqhd', pr, v)

    o = lax.map(block, (jnp.arange(nb), qb, cb))
    return jnp.moveaxis(o, 0, 1).reshape(Bsz, T, FX_HEADS, FX_HD)


def fox_decode(q, k, v, logf, cache_k, cache_v, cache_logf, page_table):
    S = q.shape[1]
    P = page_table.shape[1] * PAGE_SIZE
    mask = jnp.arange(P + S)[None, :] <= (P + jnp.arange(S))[:, None]

    def one(args):
        qi, ki, vi, lfi, pages = args
        kk = jnp.concatenate([cache_k[pages].reshape(P, FX_HEADS, FX_HD), ki], 0)
        vv = jnp.concatenate([cache_v[pages].reshape(P, FX_HEADS, FX_HD), vi], 0)
        lp = cache_logf[pages].reshape(P, FX_HEADS).astype(jnp.float32)
        cum = jnp.cumsum(jnp.concatenate([lp, lfi.astype(jnp.float32)], 0), axis=0)
        bias = cum[P:].T[:, :, None] - cum.T[:, None, :]
        s = jnp.einsum('qhd,khd->hqk', qi, kk, preferred_element_type=jnp.float32) * FX_SCALE + bias
        s = jnp.where(mask, s, -jnp.inf)
        pr = jax.nn.softmax(s, axis=-1).astype(vv.dtype)
        return jnp.einsum('hqk,khd->qhd', pr, vv)

    return lax.map(one, (q, k, v, logf, page_table))


def routed_moe(h, w_router, b_router, w_exp_gate, w_exp_up, w_exp_down):
    T, D = h.shape
    E = N_EXPERTS
    s = jax.nn.sigmoid(jnp.einsum('td,de->te', h, w_router, preferred_element_type=jnp.float32))
    sb = s + b_router.astype(jnp.float32)
    grp_score = jnp.sum(lax.top_k(sb.reshape(T, N_GROUPS, E // N_GROUPS), 2)[0], -1)
    _, gidx = lax.top_k(grp_score, TOPK_GROUPS)
    gmask = jnp.sum(jax.nn.one_hot(gidx, N_GROUPS, dtype=jnp.float32), 1) > 0
    sb = jnp.where(jnp.repeat(gmask, E // N_GROUPS, axis=1), sb, -jnp.inf)
    _, idx = lax.top_k(sb, TOP_K)
    w = jnp.take_along_axis(s, idx, axis=1)
    w = w / jnp.sum(w, -1, keepdims=True) * ROUTED_SCALE
    A = T * TOP_K
    blk = MOE_BLOCK if A >= E * MOE_BLOCK else MOE_SMALL_BLOCK
    nb = -(-A // blk) + E
    flat_e = idx.reshape(A).astype(jnp.int32)
    flat_t = jnp.repeat(jnp.arange(T, dtype=jnp.int32), TOP_K)
    flat_w = w.reshape(A)
    order = jnp.argsort(flat_e)
    se, st, sw = flat_e[order], flat_t[order], flat_w[order]
    counts = jnp.bincount(flat_e, length=E).astype(jnp.int32)
    padded = (counts + blk - 1) // blk * blk
    pad_end = jnp.cumsum(padded)
    pad_start = pad_end - padded
    grp_start = jnp.cumsum(counts) - counts
    dest = pad_start[se] + jnp.arange(A, dtype=jnp.int32) - grp_start[se]
    buf_t = jnp.zeros((nb * blk,), jnp.int32).at[dest].set(st)
    buf_w = jnp.zeros((nb * blk,), jnp.float32).at[dest].set(sw)
    blk_e = jnp.minimum(jnp.searchsorted(pad_end, jnp.arange(nb, dtype=jnp.int32) * blk, side='right'), E - 1)

    def expert_block(args):
        tok, gw, e = args
        xb = h[tok]
        y = swiglu(xb, w_exp_gate[e], w_exp_up[e], w_exp_down[e])
        return y * gw[:, None].astype(y.dtype)

    yb = lax.map(expert_block, (buf_t.reshape(nb, blk), buf_w.reshape(nb, blk), blk_e))
    return jnp.zeros_like(h).at[buf_t].add(yb.reshape(nb * blk, D))


def trunk_layer(x, p, C0, n0, m0, fox_attend, w_in, b_ml_i, b_ml_f, b_fx_f, ml_norm_g, w_br_ml, w_br_fx, w_out,
                ln1_g, ln1_b, w_router, b_router, w_exp_gate, w_exp_up, w_exp_down,
                w_sh_gate, w_sh_up, w_sh_down, w_ple_gate, w_ple_proj, ln2_g, ln2_b):
    B, T, _ = x.shape
    z = jnp.einsum('btd,de->bte', x, w_in)
    mq, mk, mv, mo, mi, mf, fq, fk, fv, ff, ga, gb = jnp.split(z, _split_points(), axis=-1)
    q_ml = mq.reshape(B, T, ML_HEADS, ML_DQK)
    k_ml = mk.reshape(B, T, ML_HEADS, ML_DQK) * (ML_DQK ** -0.5)
    v_ml = mv.reshape(B, T, ML_HEADS, ML_DV)
    li = mi.astype(jnp.float32) + b_ml_i.astype(jnp.float32)
    lf = jax.nn.log_sigmoid(mf.astype(jnp.float32) + b_ml_f.astype(jnp.float32))
    h_ml, (C, n, m) = mlstm(q_ml, k_ml, v_ml, li, lf, C0, n0, m0)
    hm = h_ml * lax.rsqrt(jnp.mean(jnp.square(h_ml), -1, keepdims=True) + LN_EPS)
    hm = hm.reshape(B, T, ML_W_V) * ml_norm_g.astype(jnp.float32) * jax.nn.sigmoid(mo.astype(jnp.float32))
    q_fx = fq.reshape(B, T, FX_HEADS, FX_HD)
    k_fx = fk.reshape(B, T, FX_HEADS, FX_HD)
    v_fx = fv.reshape(B, T, FX_HEADS, FX_HD)
    lf_fx = jax.nn.log_sigmoid(ff.astype(jnp.float32) + b_fx_f.astype(jnp.float32))
    h_fx = fox_attend(q_fx, k_fx, v_fx, lf_fx)
    br_ml = hm.astype(x.dtype) @ w_br_ml
    br_fx = h_fx.reshape(B, T, FX_W) @ w_br_fx
    merged = jax.nn.sigmoid(ga) * br_ml + jax.nn.sigmoid(gb) * br_fx
    h1 = layer_norm(DEEPNORM_ALPHA * x + merged @ w_out, ln1_g, ln1_b)
    flat = h1.reshape(B * T, D_MODEL)
    ffn = routed_moe(flat, w_router, b_router, w_exp_gate, w_exp_up, w_exp_down) + swiglu(flat, w_sh_gate, w_sh_up, w_sh_down)
    ple = jax.nn.sigmoid(h1 @ w_ple_gate) * (p @ w_ple_proj)
    y = layer_norm(DEEPNORM_ALPHA * h1 + ffn.reshape(B, T, D_MODEL) + ple, ln2_g, ln2_b)
    return y, k_fx, v_fx, lf_fx, C, n, m


def setup_inputs(seed: int = 0) -> dict:
    key = jax.random.key(seed)
    ks = iter(jax.random.split(key, 48))
    f32 = jnp.float32

    def nrm(shape, scale=1.0):
        return jax.random.normal(next(ks), shape, f32) * scale

    n_pages = PAST_LEN // PAGE_SIZE
    n_pool = (DEC_BATCH * n_pages * 5 + 3) // 4
    beta = (8.0 * DEPTH) ** -0.25
    Lr = DEPTH
    return {
        'x_prompt': nrm((BATCH, SEQ, D_MODEL)),
        'x_sample': nrm((DEC_BATCH, DEC_SEQ, D_MODEL)),
        'p_prompt': nrm((DEPTH, BATCH, SEQ, PLE_DIM)),
        'p_sample': nrm((DEPTH, DEC_BATCH, DEC_SEQ, PLE_DIM)),
        'cache_k': nrm((DEPTH, n_pool, PAGE_SIZE, FX_HEADS, FX_HD)),
        'cache_v': nrm((DEPTH, n_pool, PAGE_SIZE, FX_HEADS, FX_HD)),
        'cache_logf': jax.nn.log_sigmoid(4.5 + nrm((DEPTH, n_pool, PAGE_SIZE, FX_HEADS))),
        'state_C': nrm((DEPTH, DEC_BATCH, ML_HEADS, ML_DQK, ML_DV), 0.5),
        'state_n': nrm((DEPTH, DEC_BATCH, ML_HEADS, ML_DQK), 0.5),
        'state_m': nrm((DEPTH, DEC_BATCH, ML_HEADS)),
        'page_table': jax.random.permutation(next(ks), n_pool)[:DEC_BATCH * n_pages].reshape(DEC_BATCH, n_pages).astype(jnp.int32),
        'w_in': nrm((Lr, D_MODEL, N_IN), D_MODEL ** -0.5),
        'b_ml_i': nrm((Lr, ML_HEADS), 0.1),
        'b_ml_f': jnp.linspace(3.0, 6.0, ML_HEADS, dtype=f32) + nrm((Lr, ML_HEADS), 0.1),
        'b_fx_f': jnp.linspace(3.0, 6.0, FX_HEADS, dtype=f32) + nrm((Lr, FX_HEADS), 0.1),
        'ml_norm_g': 1.0 + nrm((Lr, ML_W_V), 0.02),
        'w_br_ml': nrm((Lr, ML_W_V, D_MODEL), ML_W_V ** -0.5),
        'w_br_fx': nrm((Lr, FX_W, D_MODEL), FX_W ** -0.5),
        'w_out': nrm((Lr, D_MODEL, D_MODEL), beta * D_MODEL ** -0.5),
        'ln1_g': 1.0 + nrm((Lr, D_MODEL), 0.02),
        'ln1_b': nrm((Lr, D_MODEL), 0.02),
        'w_router': nrm((Lr, D_MODEL, N_EXPERTS), D_MODEL ** -0.5),
        'b_router': nrm((Lr, N_EXPERTS), 0.01),
        'w_exp_gate': nrm((Lr, N_EXPERTS, D_MODEL, D_EXPERT), D_MODEL ** -0.5),
        'w_exp_up': nrm((Lr, N_EXPERTS, D_MODEL, D_EXPERT), D_MODEL ** -0.5),
        'w_exp_down': nrm((Lr, N_EXPERTS, D_EXPERT, D_MODEL), beta * D_EXPERT ** -0.5),
        'w_sh_gate': nrm((Lr, D_MODEL, D_SHARED), D_MODEL ** -0.5),
        'w_sh_up': nrm((Lr, D_MODEL, D_SHARED), D_MODEL ** -0.5),
        'w_sh_down': nrm((Lr, D_SHARED, D_MODEL), beta * D_SHARED ** -0.5),
        'w_ple_gate': nrm((Lr, D_MODEL, D_MODEL), D_MODEL ** -0.5),
        'w_ple_proj': nrm((Lr, PLE_DIM, D_MODEL), beta * PLE_DIM ** -0.5),
        'ln2_g': 1.0 + nrm((Lr, D_MODEL), 0.02),
        'ln2_b': nrm((Lr, D_MODEL), 0.02),
    }


def reference(x_prompt, x_sample, p_prompt, p_sample, cache_k, cache_v, cache_logf, state_C, state_n, state_m, page_table,
              w_in, b_ml_i, b_ml_f, b_fx_f, ml_norm_g, w_br_ml, w_br_fx, w_out, ln1_g, ln1_b,
              w_router, b_router, w_exp_gate, w_exp_up, w_exp_down, w_sh_gate, w_sh_up, w_sh_down,
              w_ple_gate, w_ple_proj, ln2_g, ln2_b):
    yp, ys = x_prompt, x_sample
    Bp = x_prompt.shape[0]
    sp_lists = [[] for _ in range(6)]
    ss_lists = [[] for _ in range(6)]
    for i in range(DEPTH):
        lw = (w_in[i], b_ml_i[i], b_ml_f[i], b_fx_f[i], ml_norm_g[i], w_br_ml[i], w_br_fx[i], w_out[i],
              ln1_g[i], ln1_b[i], w_router[i], b_router[i], w_exp_gate[i], w_exp_up[i], w_exp_down[i],
              w_sh_gate[i], w_sh_up[i], w_sh_down[i], w_ple_gate[i], w_ple_proj[i], ln2_g[i], ln2_b[i])
        C0 = jnp.zeros((Bp, ML_HEADS, ML_DQK, ML_DV), jnp.float32)
        n0 = jnp.zeros((Bp, ML_HEADS, ML_DQK), jnp.float32)
        m0 = jnp.zeros((Bp, ML_HEADS), jnp.float32)
        yp, *sp = trunk_layer(yp, p_prompt[i], C0, n0, m0, fox_prompt, *lw)
        dec = functools.partial(fox_decode, cache_k=cache_k[i], cache_v=cache_v[i], cache_logf=cache_logf[i], page_table=page_table)
        ys, *ss = trunk_layer(ys, p_sample[i], state_C[i], state_n[i], state_m[i], dec, *lw)
        for lst, a in zip(sp_lists, sp):
            lst.append(a)
        for lst, a in zip(ss_lists, ss):
            lst.append(a)
    k_prompt, v_prompt, logf_prompt, C_prompt, n_prompt, m_prompt = [jnp.stack(l) for l in sp_lists]
    k_sample, v_sample, logf_sample, C_sample, n_sample, m_sample = [jnp.stack(l) for l in ss_lists]
    return (yp, ys, k_prompt, v_prompt, logf_prompt, C_prompt, n_prompt, m_prompt,
            k_sample, v_sample, logf_sample, C_sample, n_sample, m_sample)
```

```python
import functools

import jax
import jax.numpy as jnp
from jax import lax
from jax.experimental import pallas as pl
from jax.experimental.pallas import tpu as pltpu

F32 = jnp.float32
BF16 = jnp.bfloat16

D_MODEL = 2048
ML_HEADS = 4
ML_DQK = 256
ML_DV = 512
ML_CHUNK = 128
ML_W_QK = ML_HEADS * ML_DQK
ML_W_V = ML_HEADS * ML_DV
FX_HEADS = 16
FX_HD = 128
FX_W = FX_HEADS * FX_HD
FX_SCALE = FX_HD ** -0.5
PAGE_SIZE = 128
N_EXPERTS = 256
TOP_K = 8
N_GROUPS = 8
TOPK_GROUPS = 4
D_EXPERT = 512
ROUTED_SCALE = 2.5
LN_EPS = 1e-5
DEEPNORM_ALPHA = 2.0 ** 0.25

LANES = 128
SUBLANES = 8
VMEM_LIMIT = 56 * 1024 * 1024
MOE_TILE = 128
FOX_BLOCK = 1024


def _params(*sem):
    return pltpu.CompilerParams(dimension_semantics=sem, vmem_limit_bytes=VMEM_LIMIT)


def _row_tile(m, target):
    if m <= target:
        return m
    best = max(t for t in range(LANES, target + 1, LANES) if m % t == 0)
    assert m % best == 0
    return best


def _dot(a, b):
    return jnp.dot(a, b, preferred_element_type=F32)


def _dot_nt(a, b):
    return lax.dot_general(a, b, (((1,), (1,)), ((), ())), preferred_element_type=F32)


def _split3(x):
    hi = x.astype(BF16)
    r1 = x - hi.astype(F32)
    mid = r1.astype(BF16)
    lo = (r1 - mid.astype(F32)).astype(BF16)
    return hi, mid, lo


def _layer_norm(y, g, b):
    mu = jnp.mean(y, axis=-1, keepdims=True)
    d = y - mu
    var = jnp.mean(d * d, axis=-1, keepdims=True)
    return d * lax.rsqrt(var + LN_EPS) * g + b


def _mm_body(a_ref, b_ref, *o_refs, scale):
    acc = _dot(a_ref[...], b_ref[...])
    if scale != 1.0:
        acc = acc * scale
    for o in o_refs:
        o[...] = acc.astype(o.dtype)


def matmul(a, b, out_dtypes=(F32,), tm=512, tn=512, scale=1.0, name="matmul"):
    M, K = a.shape
    N = b.shape[1]
    tm, tn = _row_tile(M, tm), _row_tile(N, tn)
    outs = pl.pallas_call(
        functools.partial(_mm_body, scale=scale),
        out_shape=[jax.ShapeDtypeStruct((M, N), d) for d in out_dtypes],
        grid=(M // tm, N // tn),
        in_specs=[pl.BlockSpec((tm, K), lambda i, j: (i, 0)),
                  pl.BlockSpec((K, tn), lambda i, j: (0, j))],
        out_specs=[pl.BlockSpec((tm, tn), lambda i, j: (i, j)) for _ in out_dtypes],
        compiler_params=_params("parallel", "parallel"),
        name=name,
    )(a, b)
    return outs


def _pair_body(a1_ref, b1_ref, a2_ref, b2_ref, g1_ref, g2_ref, o_ref, *, mode):
    y1 = _dot(a1_ref[...], b1_ref[...])
    y2 = _dot(a2_ref[...], b2_ref[...])
    if mode == "merge":
        out = jax.nn.sigmoid(g1_ref[...]) * y1 + jax.nn.sigmoid(g2_ref[...]) * y2
    else:
        out = jax.nn.sigmoid(y1) * y2
    o_ref[...] = out.astype(o_ref.dtype)


def gated_merge(a1, b1, a2, b2, gates, out_dtype, tm=512, tn=512):
    M, K = a1.shape
    N = b1.shape[1]
    tm, tn = _row_tile(M, tm), _row_tile(N, tn)
    nj = N // tn
    return pl.pallas_call(
        functools.partial(_pair_body, mode="merge"),
        out_shape=jax.ShapeDtypeStruct((M, N), out_dtype),
        grid=(M // tm, nj),
        in_specs=[pl.BlockSpec((tm, K), lambda i, j: (i, 0)),
                  pl.BlockSpec((K, tn), lambda i, j: (0, j)),
                  pl.BlockSpec((tm, K), lambda i, j: (i, 0)),
                  pl.BlockSpec((K, tn), lambda i, j: (0, j)),
                  pl.BlockSpec((tm, tn), lambda i, j: (i, j)),
                  pl.BlockSpec((tm, tn), lambda i, j: (i, j + nj))],
        out_specs=pl.BlockSpec((tm, tn), lambda i, j: (i, j)),
        compiler_params=_params("parallel", "parallel"),
        name="gated_merge",
    )(a1, b1, a2, b2, gates, gates)


def _ple_body(a1_ref, b1_ref, a2_ref, b2_ref, o_ref):
    y1 = _dot(a1_ref[...], b1_ref[...])
    y2 = _dot(a2_ref[...], b2_ref[...])
    o_ref[...] = (jax.nn.sigmoid(y1) * y2).astype(o_ref.dtype)


def ple_embed(h, w_gate, p, w_proj, tm=512, tn=512):
    M, K1 = h.shape
    K2 = p.shape[1]
    N = w_gate.shape[1]
    tm, tn = _row_tile(M, tm), _row_tile(N, tn)
    return pl.pallas_call(
        _ple_body,
        out_shape=jax.ShapeDtypeStruct((M, N), F32),
        grid=(M // tm, N // tn),
        in_specs=[pl.BlockSpec((tm, K1), lambda i, j: (i, 0)),
                  pl.BlockSpec((K1, tn), lambda i, j: (0, j)),
                  pl.BlockSpec((tm, K2), lambda i, j: (i, 0)),
                  pl.BlockSpec((K2, tn), lambda i, j: (0, j))],
        out_specs=pl.BlockSpec((tm, tn), lambda i, j: (i, j)),
        compiler_params=_params("parallel", "parallel"),
        name="ple_embed",
    )(h, w_gate, p, w_proj)


def _outln_body(mg_ref, w_ref, x_ref, g_ref, b_ref, hf_ref, hb_ref):
    y = DEEPNORM_ALPHA * x_ref[...] + _dot(mg_ref[...], w_ref[...])
    h = _layer_norm(y, g_ref[...], b_ref[...])
    hf_ref[...] = h
    hb_ref[...] = h.astype(BF16)


def out_proj_ln(merged, w_out, x, g, b, tm=256):
    M, K = merged.shape
    N = w_out.shape[1]
    tm = _row_tile(M, tm)
    return pl.pallas_call(
        _outln_body,
        out_shape=[jax.ShapeDtypeStruct((M, N), F32), jax.ShapeDtypeStruct((M, N), BF16)],
        grid=(M // tm,),
        in_specs=[pl.BlockSpec((tm, K), lambda i: (i, 0)),
                  pl.BlockSpec((K, N), lambda i: (0, 0)),
                  pl.BlockSpec((tm, N), lambda i: (i, 0)),
                  pl.BlockSpec((1, N), lambda i: (0, 0)),
                  pl.BlockSpec((1, N), lambda i: (0, 0))],
        out_specs=[pl.BlockSpec((tm, N), lambda i: (i, 0)),
                   pl.BlockSpec((tm, N), lambda i: (i, 0))],
        compiler_params=_params("parallel"),
        name="out_proj_ln",
    )(merged, w_out, x, g, b)


def _ffn_tail_body(hb_ref, hf_ref, moe_ref, ple_ref, wg_ref, wu_ref, wd_ref, g_ref, b_ref, y_ref):
    hb = hb_ref[...]
    sg = _dot(hb, wg_ref[...])
    su = _dot(hb, wu_ref[...])
    sh = _dot((jax.nn.silu(sg) * su).astype(BF16), wd_ref[...])
    y = DEEPNORM_ALPHA * hf_ref[...] + moe_ref[...] + sh + ple_ref[...]
    y_ref[...] = _layer_norm(y, g_ref[...], b_ref[...])


def ffn_tail(h_bf, h_f32, moe, ple, wg, wu, wd, g, b, tm=256):
    M, D = h_bf.shape
    Ds = wg.shape[1]
    tm = _row_tile(M, tm)
    row = lambda i: (i, 0)
    const = lambda i: (0, 0)
    return pl.pallas_call(
        _ffn_tail_body,
        out_shape=jax.ShapeDtypeStruct((M, D), F32),
        grid=(M // tm,),
        in_specs=[pl.BlockSpec((tm, D), row), pl.BlockSpec((tm, D), row),
                  pl.BlockSpec((tm, D), row), pl.BlockSpec((tm, D), row),
                  pl.BlockSpec((D, Ds), const), pl.BlockSpec((D, Ds), const),
                  pl.BlockSpec((Ds, D), const),
                  pl.BlockSpec((1, D), const), pl.BlockSpec((1, D), const)],
        out_specs=pl.BlockSpec((tm, D), row),
        compiler_params=_params("parallel"),
        name="ffn_tail",
    )(h_bf, h_f32, moe, ple, wg, wu, wd, g, b)


def _mlstm_prompt_body(q_ref, k_ref, v_ref, o_ref, li_ref, lf_ref, g_ref,
                       hm_ref, c_out, n_out, m_out, c_s, n_s, m_s):
    c = pl.program_id(1)
    L = ML_CHUNK

    @pl.when(c == 0)
    def _init():
        c_s[...] = jnp.zeros_like(c_s)
        n_s[...] = jnp.zeros_like(n_s)
        m_s[...] = jnp.zeros_like(m_s)

    q = q_ref[...]
    k = k_ref[...] * (ML_DQK ** -0.5)
    v_bf = v_ref[...].astype(BF16)
    q_bf = q.astype(BF16)
    li_r = li_ref[0, 0]
    lf_r = lf_ref[0, 0]

    row = lax.broadcasted_iota(jnp.int32, (L, L), 0)
    col = lax.broadcasted_iota(jnp.int32, (L, L), 1)
    eye = row == col
    causal = col <= row
    lf_c = jnp.sum(jnp.where(eye, lf_r, 0.0), axis=1, keepdims=True)
    b_c = jnp.sum(jnp.where(causal, lf_r, 0.0), axis=1, keepdims=True)
    b_r = jnp.sum(jnp.where(row <= col, lf_c, 0.0), axis=0, keepdims=True)
    b_tot = jnp.sum(lf_r, axis=1, keepdims=True)

    m_prev = m_s[:, 0:1]
    dlog = jnp.where(causal, b_c - b_r + li_r, -jnp.inf)
    inter = b_c + m_prev
    m_row = jnp.maximum(inter, jnp.max(dlog, axis=1, keepdims=True))
    w_inter = jnp.exp(inter - m_row)
    s = _dot_nt(q_bf, k.astype(BF16)) * jnp.exp(dlog - m_row)
    num = w_inter * _dot(q_bf, c_s[...].astype(BF16)) + _dot(s.astype(BF16), v_bf)
    den = w_inter * jnp.sum(q * n_s[...], axis=1, keepdims=True) + jnp.sum(s, axis=1, keepdims=True)
    h = num / jnp.maximum(jnp.abs(den), jnp.exp(-m_row))

    hn = h * lax.rsqrt(jnp.mean(h * h, axis=1, keepdims=True) + LN_EPS)
    hm_ref[...] = (hn * g_ref[...] * jax.nn.sigmoid(o_ref[...])).astype(hm_ref.dtype)

    lw_r = b_tot - b_r + li_r
    m_new = jnp.maximum(b_tot + m_prev, jnp.max(lw_r, axis=1, keepdims=True))
    decay = jnp.exp(b_tot + m_prev - m_new)
    wgt_r = jnp.exp(lw_r - m_new)
    wgt_c = jnp.sum(jnp.where(eye, wgt_r, 0.0), axis=1, keepdims=True)
    wk = k * wgt_c
    c_new = decay * c_s[...] + _dot(wk.T.astype(BF16), v_bf)
    n_new = decay * n_s[...] + jnp.sum(wk, axis=0, keepdims=True)
    c_s[...] = c_new
    n_s[...] = n_new
    m_s[...] = jnp.broadcast_to(m_new, m_s.shape)

    @pl.when(c == pl.num_programs(1) - 1)
    def _final():
        c_out[0] = c_new
        n_out[0] = n_new
        m_out[0] = jnp.broadcast_to(m_new, (1, LANES))


def mlstm_prompt(z, li_rows, lf_rows, norm_g):
    T = z.shape[0]
    nc = T // ML_CHUNK
    kq = ML_W_QK // ML_DQK
    kv = 2 * ML_W_QK // ML_DV
    ko = kv + ML_HEADS
    return pl.pallas_call(
        _mlstm_prompt_body,
        out_shape=[jax.ShapeDtypeStruct((T, ML_W_V), BF16),
                   jax.ShapeDtypeStruct((ML_HEADS, ML_DQK, ML_DV), F32),
                   jax.ShapeDtypeStruct((ML_HEADS, 1, ML_DQK), F32),
                   jax.ShapeDtypeStruct((ML_HEADS, 1, LANES), F32)],
        grid=(ML_HEADS, nc),
        in_specs=[pl.BlockSpec((ML_CHUNK, ML_DQK), lambda h, c: (c, h)),
                  pl.BlockSpec((ML_CHUNK, ML_DQK), lambda h, c: (c, kq + h)),
                  pl.BlockSpec((ML_CHUNK, ML_DV), lambda h, c: (c, kv + h)),
                  pl.BlockSpec((ML_CHUNK, ML_DV), lambda h, c: (c, ko + h)),
                  pl.BlockSpec((1, 1, 1, ML_CHUNK), lambda h, c: (h, c, 0, 0)),
                  pl.BlockSpec((1, 1, 1, ML_CHUNK), lambda h, c: (h, c, 0, 0)),
                  pl.BlockSpec((1, ML_DV), lambda h, c: (0, h))],
        out_specs=[pl.BlockSpec((ML_CHUNK, ML_DV), lambda h, c: (c, h)),
                   pl.BlockSpec((1, ML_DQK, ML_DV), lambda h, c: (h, 0, 0)),
                   pl.BlockSpec((1, 1, ML_DQK), lambda h, c: (h, 0, 0)),
                   pl.BlockSpec((1, 1, LANES), lambda h, c: (h, 0, 0))],
        scratch_shapes=[pltpu.VMEM((ML_DQK, ML_DV), F32),
                        pltpu.VMEM((1, ML_DQK), F32),
                        pltpu.VMEM((1, LANES), F32)],
        compiler_params=_params("parallel", "arbitrary"),
        name="mlstm_prompt",
    )(z, z, z, z, li_rows, lf_rows, norm_g)


def _mlstm_decode_body(c_ref, n_ref, q_ref, k_ref, v_ref, o_ref, gate_ref, g_ref,
                       hm_ref, c_out, n_out, m_out):
    for h in range(ML_HEADS):
        C = c_ref[0, h]
        n_c = n_ref[0, h]
        q_c = q_ref[0, h]
        k_c = k_ref[0, h] * (ML_DQK ** -0.5)
        v_r = v_ref[0, h]
        gt = gate_ref[0, h]
        li, lf, m0 = gt[:, 0:1], gt[:, 1:2], gt[:, 2:3]
        inter = lf + m0
        m_new = jnp.maximum(inter, li)
        w_inter = jnp.exp(inter - m_new)
        w_in = jnp.exp(li - m_new)
        qk = jnp.sum(q_c * k_c, axis=0, keepdims=True) * w_in
        qc = jnp.sum(C * q_c, axis=0, keepdims=True)
        num = w_inter * qc + qk * v_r
        den = w_inter * jnp.sum(q_c * n_c, axis=0, keepdims=True) + qk
        hh = num / jnp.maximum(jnp.abs(den), jnp.exp(-m_new))
        hn = hh * lax.rsqrt(jnp.mean(hh * hh, axis=1, keepdims=True) + LN_EPS)
        hm_ref[0, h] = (hn * g_ref[h] * jax.nn.sigmoid(o_ref[0, h])).astype(hm_ref.dtype)
        wk = k_c * w_in
        c_out[0, h] = w_inter * C + wk * v_r
        n_out[0, h] = w_inter * n_c + wk
        m_out[0, h] = jnp.broadcast_to(m_new, (1, LANES))


def mlstm_decode(state_c, state_n_col, q_col, k_col, v_row, o_row, gates, norm_g):
    B = state_c.shape[0]
    H = ML_HEADS
    blk4 = lambda *s: pl.BlockSpec((1, H) + s, lambda b: (b, 0, 0, 0))
    return pl.pallas_call(
        _mlstm_decode_body,
        out_shape=[jax.ShapeDtypeStruct((B, H, 1, ML_DV), BF16),
                   jax.ShapeDtypeStruct((B, H, ML_DQK, ML_DV), F32),
                   jax.ShapeDtypeStruct((B, H, ML_DQK, 1), F32),
                   jax.ShapeDtypeStruct((B, H, 1, LANES), F32)],
        grid=(B,),
        in_specs=[blk4(ML_DQK, ML_DV), blk4(ML_DQK, 1), blk4(ML_DQK, 1), blk4(ML_DQK, 1),
                  blk4(1, ML_DV), blk4(1, ML_DV), blk4(1, LANES),
                  pl.BlockSpec((H, 1, ML_DV), lambda b: (0, 0, 0))],
        out_specs=[blk4(1, ML_DV), blk4(ML_DQK, ML_DV), blk4(ML_DQK, 1), blk4(1, LANES)],
        compiler_params=_params("parallel"),
        name="mlstm_decode",
    )(state_c, state_n_col, q_col, k_col, v_row, o_row, gates, norm_g)


def _fox_prompt_body(qi_tab, ki_tab, q_ref, k_ref, v_ref, kb_ref, kbq_ref, o_ref, m_s, l_s, acc_s):
    p = pl.program_id(1)
    qi = qi_tab[p]
    ki = ki_tab[p]
    tb = q_ref.shape[0]

    @pl.when(ki == 0)
    def _init():
        m_s[...] = jnp.full_like(m_s, -jnp.inf)
        l_s[...] = jnp.zeros_like(l_s)
        acc_s[...] = jnp.zeros_like(acc_s)

    kb = kb_ref[0] - kbq_ref[0][:, 0:1]
    t = _dot_nt(q_ref[...], k_ref[...]) + kb

    def update(t):
        m_prev = m_s[...]
        m_new = jnp.maximum(m_prev, jnp.max(t, axis=1, keepdims=True))
        alpha = jnp.exp(m_prev - m_new)
        pr = jnp.exp(t - m_new)
        l_s[...] = alpha * l_s[...] + jnp.sum(pr, axis=1, keepdims=True)
        acc_s[...] = alpha * acc_s[...] + _dot(pr.astype(BF16), v_ref[...])
        m_s[...] = m_new

    @pl.when(ki < qi)
    def _off_diag():
        update(t)

    @pl.when(ki == qi)
    def _diag():
        row = lax.broadcasted_iota(jnp.int32, (tb, tb), 0)
        col = lax.broadcasted_iota(jnp.int32, (tb, tb), 1)
        update(jnp.where(col <= row, t, -jnp.inf))
        o_ref[...] = (acc_s[...] / l_s[...]).astype(o_ref.dtype)


def fox_prompt(q, k, v, neg_cum, tb=FOX_BLOCK):
    T = q.shape[0]
    tb = min(tb, T)
    nb = T // tb
    pairs = [(i, j) for i in range(nb) for j in range(i + 1)]
    qi_tab = jnp.array([a for a, _ in pairs], jnp.int32)
    ki_tab = jnp.array([b for _, b in pairs], jnp.int32)
    grid_spec = pltpu.PrefetchScalarGridSpec(
        num_scalar_prefetch=2,
        grid=(FX_HEADS, len(pairs)),
        in_specs=[pl.BlockSpec((tb, FX_HD), lambda h, p, qt, kt: (qt[p], h)),
                  pl.BlockSpec((tb, FX_HD), lambda h, p, qt, kt: (kt[p], h)),
                  pl.BlockSpec((tb, FX_HD), lambda h, p, qt, kt: (kt[p], h)),
                  pl.BlockSpec((1, 1, tb), lambda h, p, qt, kt: (h, 0, kt[p])),
                  pl.BlockSpec((1, 1, tb), lambda h, p, qt, kt: (h, 0, qt[p]))],
        out_specs=pl.BlockSpec((tb, FX_HD), lambda h, p, qt, kt: (qt[p], h)),
        scratch_shapes=[pltpu.VMEM((tb, 1), F32), pltpu.VMEM((tb, 1), F32),
                        pltpu.VMEM((tb, FX_HD), F32)])
    return pl.pallas_call(
        _fox_prompt_body,
        out_shape=jax.ShapeDtypeStruct((T, FX_W), BF16),
        grid_spec=grid_spec,
        compiler_params=_params("parallel", "arbitrary"),
        name="fox_prompt",
    )(qi_tab, ki_tab, q, k, v, neg_cum, neg_cum)


def _fox_decode_body(pt_ref, qbd_ref, kc_ref, vc_ref, lfc_ref, knew_ref, vnew_ref, lfnew_ref,
                     expand_ref, o_ref, m_s, l_s, acc_s, carry_s, pad_s):
    pg = pl.program_id(1)
    n_pages = pl.num_programs(1)
    R = PAGE_SIZE
    qbd = qbd_ref[0]
    expand = expand_ref[...]

    def bcast_heads(x):
        hi, mid, lo = _split3(x)
        return _dot(hi, expand) + _dot(mid, expand) + _dot(lo, expand)

    @pl.when(pg == 0)
    def _init():
        s_self = _dot(knew_ref[0], qbd)[0:1]
        m_s[...] = s_self
        l_s[...] = jnp.ones_like(l_s)
        acc_s[...] = vnew_ref[0]
        carry_s[...] = jnp.zeros_like(carry_s)
        pad_s[...] = jnp.zeros_like(pad_s)

    pad_s[:, 0:FX_HEADS] = lfc_ref[0]
    lp = pad_s[...]
    row = lax.broadcasted_iota(jnp.int32, (R, R), 0)
    col = lax.broadcasted_iota(jnp.int32, (R, R), 1)
    upper = jnp.where(col > row, 1.0, 0.0).astype(BF16)
    hi, mid, lo = _split3(lp)
    later = _dot(upper, hi) + _dot(upper, mid) + _dot(upper, lo)
    bias = later + carry_s[...] + lfnew_ref[0]
    carry_s[...] = carry_s[...] + jnp.sum(lp, axis=0, keepdims=True)

    t = _dot(kc_ref[0].astype(BF16), qbd) + bias
    m_prev = m_s[...]
    m_new = jnp.maximum(m_prev, jnp.max(t, axis=0, keepdims=True))
    alpha = jnp.exp(m_prev - m_new)
    pr = jnp.exp(t - m_new)
    l_s[...] = alpha * l_s[...] + jnp.sum(pr, axis=0, keepdims=True)
    m_s[...] = m_new
    pv = _dot(pr.astype(BF16), expand) * vc_ref[0]
    part = pv[0:SUBLANES]
    for g in range(1, R // SUBLANES):
        part = part + pv[g * SUBLANES:(g + 1) * SUBLANES]
    acc_s[...] = acc_s[...] * bcast_heads(alpha) + part

    @pl.when(pg == n_pages - 1)
    def _final():
        tot = jnp.sum(acc_s[...], axis=0, keepdims=True)
        o_ref[0] = (tot / bcast_heads(l_s[...])).astype(o_ref.dtype)


def fox_decode(page_table, qbd, cache_k, cache_v, cache_logf, k_new8, v_new8, lf_new, expand):
    B, n_pages = page_table.shape
    W = FX_W
    page = lambda b, p, pt: (pt[b * n_pages + n_pages - 1 - p], 0, 0)
    seq = lambda b, p, pt: (b, 0, 0)
    grid_spec = pltpu.PrefetchScalarGridSpec(
        num_scalar_prefetch=1,
        grid=(B, n_pages),
        in_specs=[pl.BlockSpec((1, W, LANES), seq),
                  pl.BlockSpec((1, PAGE_SIZE, W), page),
                  pl.BlockSpec((1, PAGE_SIZE, W), page),
                  pl.BlockSpec((1, PAGE_SIZE, FX_HEADS), page),
                  pl.BlockSpec((1, SUBLANES, W), seq),
                  pl.BlockSpec((1, SUBLANES, W), seq),
                  pl.BlockSpec((1, 1, LANES), seq),
                  pl.BlockSpec((LANES, W), lambda b, p, pt: (0, 0))],
        out_specs=pl.BlockSpec((1, 1, W), seq),
        scratch_shapes=[pltpu.VMEM((1, LANES), F32), pltpu.VMEM((1, LANES), F32),
                        pltpu.VMEM((SUBLANES, W), F32), pltpu.VMEM((1, LANES), F32),
                        pltpu.VMEM((PAGE_SIZE, LANES), F32)])
    return pl.pallas_call(
        _fox_decode_body,
        out_shape=jax.ShapeDtypeStruct((B, 1, W), BF16),
        grid_spec=grid_spec,
        compiler_params=_params("parallel", "arbitrary"),
        name="fox_decode",
    )(page_table.reshape(-1), qbd, cache_k, cache_v, cache_logf, k_new8, v_new8, lf_new, expand)


def _router_body(h_ref, whi_ref, wlo_ref, s_ref):
    h = h_ref[...]
    h_hi = h.astype(BF16)
    h_lo = (h - h_hi.astype(F32)).astype(BF16)
    whi = whi_ref[...]
    logits = _dot(h_hi, whi) + _dot(h_lo, whi) + _dot(h_hi, wlo_ref[...])
    s_ref[...] = jax.nn.sigmoid(logits)


def router_scores(h, w_hi, w_lo, tm=640):
    M, D = h.shape
    E = w_hi.shape[1]
    tm = _row_tile(M, tm)
    return pl.pallas_call(
        _router_body,
        out_shape=jax.ShapeDtypeStruct((M, E), F32),
        grid=(M // tm,),
        in_specs=[pl.BlockSpec((tm, D), lambda i: (i, 0)),
                  pl.BlockSpec((D, E), lambda i: (0, 0)),
                  pl.BlockSpec((D, E), lambda i: (0, 0))],
        out_specs=pl.BlockSpec((tm, E), lambda i: (i, 0)),
        compiler_params=_params("parallel"),
        name="router_scores",
    )(h, w_hi, w_lo)


def _moe_body(te_ref, tr_ref, tf_ref, nv_ref, x_ref, gw_ref, wg_ref, wu_ref, wd_ref, y_ref,
              wg_s, wu_s, wd_s):
    i = pl.program_id(0)

    @pl.when(jnp.logical_and(i < nv_ref[0], tf_ref[i] == 1))
    def _load_expert():
        wg_s[...] = wg_ref[0].astype(BF16)
        wu_s[...] = wu_ref[0].astype(BF16)
        wd_s[...] = wd_ref[0].astype(BF16)

    @pl.when(i < nv_ref[0])
    def _tile():
        x = x_ref[...]
        g = _dot(x, wg_s[...])
        u = _dot(x, wu_s[...])
        y = _dot((jax.nn.silu(g) * u).astype(BF16), wd_s[...])
        y_ref[...] = y * gw_ref[...]


def moe_experts(xs, gate_w, tile_expert, tile_row, tile_first, n_valid, w_gate, w_up, w_down):
    R, D = xs.shape
    tm = MOE_TILE
    nt = R // tm
    De = w_gate.shape[2]
    grid_spec = pltpu.PrefetchScalarGridSpec(
        num_scalar_prefetch=4,
        grid=(nt,),
        in_specs=[pl.BlockSpec((tm, D), lambda i, te, tr, tf, nv: (tr[i], 0)),
                  pl.BlockSpec((tm, 1), lambda i, te, tr, tf, nv: (tr[i], 0)),
                  pl.BlockSpec((1, D, De), lambda i, te, tr, tf, nv: (te[i], 0, 0)),
                  pl.BlockSpec((1, D, De), lambda i, te, tr, tf, nv: (te[i], 0, 0)),
                  pl.BlockSpec((1, De, D), lambda i, te, tr, tf, nv: (te[i], 0, 0))],
        out_specs=pl.BlockSpec((tm, D), lambda i, te, tr, tf, nv: (tr[i], 0)),
        scratch_shapes=[pltpu.VMEM((D, De), BF16), pltpu.VMEM((D, De), BF16),
                        pltpu.VMEM((De, D), BF16)])
    return pl.pallas_call(
        _moe_body,
        out_shape=jax.ShapeDtypeStruct((R, D), F32),
        grid_spec=grid_spec,
        compiler_params=_params("arbitrary"),
        name="moe_experts",
    )(tile_expert, tile_row, tile_first, n_valid, xs, gate_w, w_gate, w_up, w_down)


def routed_moe(h_f32, h_bf, w_r_hi, w_r_lo, b_router, w_gate, w_up, w_down):
    T, D = h_f32.shape
    E = N_EXPERTS
    s = router_scores(h_f32, w_r_hi, w_r_lo)
    sb = s + b_router.astype(F32)
    grp_score = jnp.sum(lax.top_k(sb.reshape(T, N_GROUPS, E // N_GROUPS), 2)[0], -1)
    _, gidx = lax.top_k(grp_score, TOPK_GROUPS)
    gmask = jnp.sum(jax.nn.one_hot(gidx, N_GROUPS, dtype=F32), 1) > 0
    sb = jnp.where(jnp.repeat(gmask, E // N_GROUPS, axis=1), sb, -jnp.inf)
    _, idx = lax.top_k(sb, TOP_K)
    w = jnp.take_along_axis(s, idx, axis=1)
    w = w / jnp.sum(w, -1, keepdims=True) * ROUTED_SCALE

    A = T * TOP_K
    tm = MOE_TILE
    nt = -(-A // tm) + E
    flat_e = idx.reshape(A).astype(jnp.int32)
    flat_t = jnp.repeat(jnp.arange(T, dtype=jnp.int32), TOP_K)
    order = jnp.argsort(flat_e)
    se, st = flat_e[order], flat_t[order]
    counts = jnp.bincount(flat_e, length=E).astype(jnp.int32)
    padded = (counts + tm - 1) // tm * tm
    pad_end = jnp.cumsum(padded)
    pad_start = pad_end - padded
    grp_start = jnp.cumsum(counts) - counts
    dest = pad_start[se] + jnp.arange(A, dtype=jnp.int32) - grp_start[se]
    buf_t = jnp.zeros((nt * tm,), jnp.int32).at[dest].set(st)
    buf_w = jnp.zeros((nt * tm,), F32).at[dest].set(w.reshape(A)[order])
    pos = jnp.zeros((A,), jnp.int32).at[order].set(dest)
    n_valid = (pad_end[-1] // tm).astype(jnp.int32)
    tile_ids = jnp.arange(nt, dtype=jnp.int32)
    tile_row = jnp.minimum(tile_ids, n_valid - 1)
    tile_expert = jnp.minimum(jnp.searchsorted(pad_end, tile_row * tm, side='right'), E - 1).astype(jnp.int32)
    tile_first = jnp.concatenate([jnp.ones((1,), jnp.int32),
                                  (tile_expert[1:] != tile_expert[:-1]).astype(jnp.int32)])

    xs = h_bf[buf_t]
    yb = moe_experts(xs, buf_w[:, None], tile_expert, tile_row, tile_first, n_valid.reshape(1),
                     w_gate, w_up, w_down)
    return jnp.sum(yb[pos].reshape(T, TOP_K, D), axis=1)


def _prep_weights(w_in, b_ml_i, b_ml_f, b_fx_f, w_br_ml, w_br_fx, w_out, w_router,
                  w_sh_gate, w_sh_up, w_sh_down, w_ple_gate, w_ple_proj):
    o_ml = 2 * ML_W_QK + 2 * ML_W_V
    o_fx = o_ml + 2 * ML_HEADS
    o_ff = o_fx + 3 * FX_W
    o_g = o_ff + FX_HEADS
    wb = w_in.astype(BF16)
    small = jnp.concatenate([wb[:, o_ml:o_fx], wb[:, o_ff:o_g]], axis=1)
    small = jnp.pad(small, ((0, 0), (0, LANES - small.shape[1])))
    w_r_hi = w_router.astype(BF16)
    w_r_lo = (w_router - w_r_hi.astype(F32)).astype(BF16)
    return dict(
        w_ml=wb[:, :o_ml], w_fq=wb[:, o_fx:o_fx + FX_W], w_fk=wb[:, o_fx + FX_W:o_fx + 2 * FX_W],
        w_fv=wb[:, o_fx + 2 * FX_W:o_ff], w_g=wb[:, o_g:], w_small=small,
        w_br_ml=w_br_ml.astype(BF16), w_br_fx=w_br_fx.astype(BF16), w_out=w_out.astype(BF16),
        w_r_hi=w_r_hi, w_r_lo=w_r_lo,
        w_sh_gate=w_sh_gate.astype(BF16), w_sh_up=w_sh_up.astype(BF16), w_sh_down=w_sh_down.astype(BF16),
        w_ple_gate=w_ple_gate.astype(BF16), w_ple_proj=w_ple_proj.astype(BF16))


def _in_proj(x_bf, W):
    (z_ml,) = matmul(x_bf, W['w_ml'], name="proj_ml")
    (fq,) = matmul(x_bf, W['w_fq'], out_dtypes=(BF16,), scale=FX_SCALE, name="proj_fq")
    fk, fk_bf = matmul(x_bf, W['w_fk'], out_dtypes=(F32, BF16), name="proj_fk")
    fv, fv_bf = matmul(x_bf, W['w_fv'], out_dtypes=(F32, BF16), name="proj_fv")
    (z_g,) = matmul(x_bf, W['w_g'], name="proj_gates")
    (z_s,) = matmul(x_bf, W['w_small'], name="proj_small")
    return z_ml, fq, fk, fk_bf, fv, fv_bf, z_g, z_s


def _small_gates(z_s, b_ml_i, b_ml_f, b_fx_f):
    H = ML_HEADS
    li = z_s[:, 0:H] + b_ml_i
    lf = jax.nn.log_sigmoid(z_s[:, H:2 * H] + b_ml_f)
    lf_fx = jax.nn.log_sigmoid(z_s[:, 2 * H:2 * H + FX_HEADS] + b_fx_f)
    return li, lf, lf_fx


def kernel(x_prompt, x_sample, p_prompt, p_sample, cache_k, cache_v, cache_logf, state_C, state_n, state_m,
           page_table, w_in, b_ml_i, b_ml_f, b_fx_f, ml_norm_g, w_br_ml, w_br_fx, w_out, ln1_g, ln1_b,
           w_router, b_router, w_exp_gate, w_exp_up, w_exp_down, w_sh_gate, w_sh_up, w_sh_down,
           w_ple_gate, w_ple_proj, ln2_g, ln2_b):
    T = x_prompt.shape[1]
    B = x_sample.shape[0]
    H = ML_HEADS
    W = _prep_weights(w_in[0], b_ml_i[0], b_ml_f[0], b_fx_f[0], w_br_ml[0], w_br_fx[0], w_out[0], w_router[0],
                      w_sh_gate[0], w_sh_up[0], w_sh_down[0], w_ple_gate[0], w_ple_proj[0])
    xp = x_prompt[0]
    xs = x_sample[:, 0]

    z_ml, fq, fk, fk_bf, fv, fv_bf, z_g, z_s = _in_proj(xp.astype(BF16), W)
    li, lf, lf_fx = _small_gates(z_s, b_ml_i[0], b_ml_f[0], b_fx_f[0])
    nc = T // ML_CHUNK
    to_rows = lambda a: a.T.reshape(H, nc, 1, ML_CHUNK)
    hm_p, C_p, n_p, m_p = mlstm_prompt(z_ml, to_rows(li), to_rows(lf), ml_norm_g)
    neg_cum = (-jnp.cumsum(lf_fx, axis=0)).T.reshape(FX_HEADS, 1, T)
    hfx_p = fox_prompt(fq, fk_bf, fv_bf, neg_cum)
    merged_p = gated_merge(hm_p, W['w_br_ml'], hfx_p, W['w_br_fx'], z_g, BF16)
    h1_p, h1b_p = out_proj_ln(merged_p, W['w_out'], xp, ln1_g, ln1_b)

    zd_ml, dq, dk, dk_bf, dv, dv_bf, zd_g, zd_s = _in_proj(xs.astype(BF16), W)
    dli, dlf, dlf_fx = _small_gates(zd_s, b_ml_i[0], b_ml_f[0], b_fx_f[0])
    col = lambda a: a.reshape(B, H, ML_DQK, 1)
    gates = jnp.stack([dli, dlf, state_m[0]], axis=-1)
    gates = jnp.pad(gates, ((0, 0), (0, 0), (0, LANES - 3))).reshape(B, H, 1, LANES)
    hm_d, C_d, n_d, m_d = mlstm_decode(
        state_C[0], col(state_n[0]), col(zd_ml[:, :ML_W_QK]), col(zd_ml[:, ML_W_QK:2 * ML_W_QK]),
        zd_ml[:, 2 * ML_W_QK:2 * ML_W_QK + ML_W_V].reshape(B, H, 1, ML_DV),
        zd_ml[:, 2 * ML_W_QK + ML_W_V:].reshape(B, H, 1, ML_DV),
        gates, ml_norm_g.reshape(H, 1, ML_DV))
    head_of_row = jnp.arange(FX_W, dtype=jnp.int32) // FX_HD
    blockdiag = head_of_row[:, None] == jnp.arange(LANES, dtype=jnp.int32)[None, :]
    qbd = jnp.where(blockdiag[None], dq[:, :, None], jnp.zeros((), BF16))
    pad8 = lambda a: jnp.pad(a[:, None, :], ((0, 0), (0, SUBLANES - 1), (0, 0)))
    lf_new = jnp.pad(dlf_fx, ((0, 0), (0, LANES - FX_HEADS)))[:, None, :]
    n_pool = cache_k.shape[1]
    hfx_d = fox_decode(page_table, qbd, cache_k[0].reshape(n_pool, PAGE_SIZE, FX_W),
                       cache_v[0].reshape(n_pool, PAGE_SIZE, FX_W), cache_logf[0],
                       pad8(dk_bf), pad8(dv), lf_new, blockdiag.T.astype(BF16))
    merged_d = gated_merge(hm_d.reshape(B, ML_W_V), W['w_br_ml'], hfx_d.reshape(B, FX_W), W['w_br_fx'], zd_g, BF16)
    h1_d, h1b_d = out_proj_ln(merged_d, W['w_out'], xs, ln1_g, ln1_b)

    h1_all = jnp.concatenate([h1_p, h1_d], axis=0)
    h1b_all = jnp.concatenate([h1b_p, h1b_d], axis=0)
    moe = routed_moe(h1_all, h1b_all, W['w_r_hi'], W['w_r_lo'], b_router[0],
                     w_exp_gate[0], w_exp_up[0], w_exp_down[0])
    ple_p = ple_embed(h1b_p, W['w_ple_gate'], p_prompt[0, 0].astype(BF16), W['w_ple_proj'])
    ple_d = ple_embed(h1b_d, W['w_ple_gate'], p_sample[0, :, 0].astype(BF16), W['w_ple_proj'])
    y_p = ffn_tail(h1b_p, h1_p, moe[:T], ple_p, W['w_sh_gate'], W['w_sh_up'], W['w_sh_down'], ln2_g, ln2_b)
    y_d = ffn_tail(h1b_d, h1_d, moe[T:], ple_d, W['w_sh_gate'], W['w_sh_up'], W['w_sh_down'], ln2_g, ln2_b)

    return (y_p[None], y_d[:, None],
            fk.reshape(1, 1, T, FX_HEADS, FX_HD), fv.reshape(1, 1, T, FX_HEADS, FX_HD), lf_fx[None, None],
            C_p[None, None], n_p.reshape(1, 1, H, ML_DQK), m_p[:, 0, 0].reshape(1, 1, H),
            dk.reshape(1, B, 1, FX_HEADS, FX_HD), dv.reshape(1, B, 1, FX_HEADS, FX_HD), dlf_fx[None, :, None],
            C_d[None], n_d.reshape(1, B, H, ML_DQK), m_d[:, :, 0, 0][None])
```

```python
import functools

import jax
import jax.numpy as jnp
from jax import lax
from jax.experimental import pallas as pl
from jax.experimental.pallas import tpu as pltpu

F32 = jnp.float32
BF16 = jnp.bfloat16

D_MODEL = 2048
ML_HEADS = 4
ML_DQK = 256
ML_DV = 512
ML_CHUNK = 128
ML_W_QK = ML_HEADS * ML_DQK
ML_W_V = ML_HEADS * ML_DV
FX_HEADS = 16
FX_HD = 128
FX_W = FX_HEADS * FX_HD
FX_SCALE = FX_HD ** -0.5
PAGE_SIZE = 128
N_EXPERTS = 256
TOP_K = 8
N_GROUPS = 8
TOPK_GROUPS = 4
D_EXPERT = 512
ROUTED_SCALE = 2.5
LN_EPS = 1e-5
DEEPNORM_ALPHA = 2.0 ** 0.25

LANES = 128
SUBLANES = 8
VMEM_LIMIT = 56 * 1024 * 1024
MOE_TILE = 128
FOX_BLOCK = 1024


def _params(*sem):
    return pltpu.CompilerParams(dimension_semantics=sem, vmem_limit_bytes=VMEM_LIMIT)


def _row_tile(m, target):
    if m <= target:
        return m
    best = max(t for t in range(LANES, target + 1, LANES) if m % t == 0)
    assert m % best == 0
    return best


def _dot(a, b):
    return jnp.dot(a, b, preferred_element_type=F32)


def _dot_nt(a, b):
    return lax.dot_general(a, b, (((1,), (1,)), ((), ())), preferred_element_type=F32)


def _split3(x):
    hi = x.astype(BF16)
    r1 = x - hi.astype(F32)
    mid = r1.astype(BF16)
    lo = (r1 - mid.astype(F32)).astype(BF16)
    return hi, mid, lo


def _layer_norm(y, g, b):
    mu = jnp.mean(y, axis=-1, keepdims=True)
    d = y - mu
    var = jnp.mean(d * d, axis=-1, keepdims=True)
    return d * lax.rsqrt(var + LN_EPS) * g + b


def _mm_body(a_ref, b_ref, *o_refs, scale):
    acc = _dot(a_ref[...], b_ref[...])
    if scale != 1.0:
        acc = acc * scale
    for o in o_refs:
        o[...] = acc.astype(o.dtype)


def matmul(a, b, out_dtypes=(F32,), tm=512, tn=512, scale=1.0, name="matmul"):
    M, K = a.shape
    N = b.shape[1]
    tm, tn = _row_tile(M, tm), _row_tile(N, tn)
    outs = pl.pallas_call(
        functools.partial(_mm_body, scale=scale),
        out_shape=[jax.ShapeDtypeStruct((M, N), d) for d in out_dtypes],
        grid=(M // tm, N // tn),
        in_specs=[pl.BlockSpec((tm, K), lambda i, j: (i, 0)),
                  pl.BlockSpec((K, tn), lambda i, j: (0, j))],
        out_specs=[pl.BlockSpec((tm, tn), lambda i, j: (i, j)) for _ in out_dtypes],
        compiler_params=_params("parallel", "parallel"),
        name=name,
    )(a, b)
    return outs


def _pair_body(a1_ref, b1_ref, a2_ref, b2_ref, g1_ref, g2_ref, o_ref, *, mode):
    y1 = _dot(a1_ref[...], b1_ref[...])
    y2 = _dot(a2_ref[...], b2_ref[...])
    if mode == "merge":
        out = jax.nn.sigmoid(g1_ref[...]) * y1 + jax.nn.sigmoid(g2_ref[...]) * y2
    else:
        out = jax.nn.sigmoid(y1) * y2
    o_ref[...] = out.astype(o_ref.dtype)


def gated_merge(a1, b1, a2, b2, gates, out_dtype, tm=512, tn=512):
    M, K = a1.shape
    N = b1.shape[1]
    tm, tn = _row_tile(M, tm), _row_tile(N, tn)
    nj = N // tn
    return pl.pallas_call(
        functools.partial(_pair_body, mode="merge"),
        out_shape=jax.ShapeDtypeStruct((M, N), out_dtype),
        grid=(M // tm, nj),
        in_specs=[pl.BlockSpec((tm, K), lambda i, j: (i, 0)),
                  pl.BlockSpec((K, tn), lambda i, j: (0, j)),
                  pl.BlockSpec((tm, K), lambda i, j: (i, 0)),
                  pl.BlockSpec((K, tn), lambda i, j: (0, j)),
                  pl.BlockSpec((tm, tn), lambda i, j: (i, j)),
                  pl.BlockSpec((tm, tn), lambda i, j: (i, j + nj))],
        out_specs=pl.BlockSpec((tm, tn), lambda i, j: (i, j)),
        compiler_params=_params("parallel", "parallel"),
        name="gated_merge",
    )(a1, b1, a2, b2, gates, gates)


def _ple_body(a1_ref, b1_ref, a2_ref, b2_ref, o_ref):
    y1 = _dot(a1_ref[...], b1_ref[...])
    y2 = _dot(a2_ref[...], b2_ref[...])
    o_ref[...] = (jax.nn.sigmoid(y1) * y2).astype(o_ref.dtype)


def ple_embed(h, w_gate, p, w_proj, tm=512, tn=512):
    M, K1 = h.shape
    K2 = p.shape[1]
    N = w_gate.shape[1]
    tm, tn = _row_tile(M, tm), _row_tile(N, tn)
    return pl.pallas_call(
        _ple_body,
        out_shape=jax.ShapeDtypeStruct((M, N), F32),
        grid=(M // tm, N // tn),
        in_specs=[pl.BlockSpec((tm, K1), lambda i, j: (i, 0)),
                  pl.BlockSpec((K1, tn), lambda i, j: (0, j)),
                  pl.BlockSpec((tm, K2), lambda i, j: (i, 0)),
                  pl.BlockSpec((K2, tn), lambda i, j: (0, j))],
        out_specs=pl.BlockSpec((tm, tn), lambda i, j: (i, j)),
        compiler_params=_params("parallel", "parallel"),
        name="ple_embed",
    )(h, w_gate, p, w_proj)


def _outln_body(mg_ref, w_ref, x_ref, g_ref, b_ref, hf_ref, hb_ref):
    y = DEEPNORM_ALPHA * x_ref[...] + _dot(mg_ref[...], w_ref[...])
    h = _layer_norm(y, g_ref[...], b_ref[...])
    hf_ref[...] = h
    hb_ref[...] = h.astype(BF16)


def out_proj_ln(merged, w_out, x, g, b, tm=256):
    M, K = merged.shape
    N = w_out.shape[1]
    tm = _row_tile(M, tm)
    return pl.pallas_call(
        _outln_body,
        out_shape=[jax.ShapeDtypeStruct((M, N), F32), jax.ShapeDtypeStruct((M, N), BF16)],
        grid=(M // tm,),
        in_specs=[pl.BlockSpec((tm, K), lambda i: (i, 0)),
                  pl.BlockSpec((K, N), lambda i: (0, 0)),
                  pl.BlockSpec((tm, N), lambda i: (i, 0)),
                  pl.BlockSpec((1, N), lambda i: (0, 0)),
                  pl.BlockSpec((1, N), lambda i: (0, 0))],
        out_specs=[pl.BlockSpec((tm, N), lambda i: (i, 0)),
                   pl.BlockSpec((tm, N), lambda i: (i, 0))],
        compiler_params=_params("parallel"),
        name="out_proj_ln",
    )(merged, w_out, x, g, b)


def _ffn_tail_body(hb_ref, hf_ref, moe_ref, ple_ref, wg_ref, wu_ref, wd_ref, g_ref, b_ref, y_ref):
    hb = hb_ref[...]
    sg = _dot(hb, wg_ref[...])
    su = _dot(hb, wu_ref[...])
    sh = _dot((jax.nn.silu(sg) * su).astype(BF16), wd_ref[...])
    y = DEEPNORM_ALPHA * hf_ref[...] + moe_ref[...] + sh + ple_ref[...]
    y_ref[...] = _layer_norm(y, g_ref[...], b_ref[...])


def ffn_tail(h_bf, h_f32, moe, ple, wg, wu, wd, g, b, tm=256):
    M, D = h_bf.shape
    Ds = wg.shape[1]
    tm = _row_tile(M, tm)
    row = lambda i: (i, 0)
    const = lambda i: (0, 0)
    return pl.pallas_call(
        _ffn_tail_body,
        out_shape=jax.ShapeDtypeStruct((M, D), F32),
        grid=(M // tm,),
        in_specs=[pl.BlockSpec((tm, D), row), pl.BlockSpec((tm, D), row),
                  pl.BlockSpec((tm, D), row), pl.BlockSpec((tm, D), row),
                  pl.BlockSpec((D, Ds), const), pl.BlockSpec((D, Ds), const),
                  pl.BlockSpec((Ds, D), const),
                  pl.BlockSpec((1, D), const), pl.BlockSpec((1, D), const)],
        out_specs=pl.BlockSpec((tm, D), row),
        compiler_params=_params("parallel"),
        name="ffn_tail",
    )(h_bf, h_f32, moe, ple, wg, wu, wd, g, b)


def _mlstm_prompt_body(q_ref, k_ref, v_ref, o_ref, li_ref, lf_ref, g_ref,
                       hm_ref, c_out, n_out, m_out, c_s, n_s, m_s):
    c = pl.program_id(1)
    L = ML_CHUNK

    @pl.when(c == 0)
    def _init():
        c_s[...] = jnp.zeros_like(c_s)
        n_s[...] = jnp.zeros_like(n_s)
        m_s[...] = jnp.zeros_like(m_s)

    q = q_ref[...]
    k = k_ref[...] * (ML_DQK ** -0.5)
    v_bf = v_ref[...].astype(BF16)
    q_bf = q.astype(BF16)
    li_r = li_ref[0, 0]
    lf_r = lf_ref[0, 0]

    row = lax.broadcasted_iota(jnp.int32, (L, L), 0)
    col = lax.broadcasted_iota(jnp.int32, (L, L), 1)
    eye = row == col
    causal = col <= row
    lf_c = jnp.sum(jnp.where(eye, lf_r, 0.0), axis=1, keepdims=True)
    b_c = jnp.sum(jnp.where(causal, lf_r, 0.0), axis=1, keepdims=True)
    b_r = jnp.sum(jnp.where(row <= col, lf_c, 0.0), axis=0, keepdims=True)
    b_tot = jnp.sum(lf_r, axis=1, keepdims=True)

    m_prev = m_s[:, 0:1]
    dlog = jnp.where(causal, b_c - b_r + li_r, -jnp.inf)
    inter = b_c + m_prev
    m_row = jnp.maximum(inter, jnp.max(dlog, axis=1, keepdims=True))
    w_inter = jnp.exp(inter - m_row)
    s = _dot_nt(q_bf, k.astype(BF16)) * jnp.exp(dlog - m_row)
    num = w_inter * _dot(q_bf, c_s[...].astype(BF16)) + _dot(s.astype(BF16), v_bf)
    den = w_inter * jnp.sum(q * n_s[...], axis=1, keepdims=True) + jnp.sum(s, axis=1, keepdims=True)
    h = num / jnp.maximum(jnp.abs(den), jnp.exp(-m_row))

    hn = h * lax.rsqrt(jnp.mean(h * h, axis=1, keepdims=True) + LN_EPS)
    hm_ref[...] = (hn * g_ref[...] * jax.nn.sigmoid(o_ref[...])).astype(hm_ref.dtype)

    lw_r = b_tot - b_r + li_r
    m_new = jnp.maximum(b_tot + m_prev, jnp.max(lw_r, axis=1, keepdims=True))
    decay = jnp.exp(b_tot + m_prev - m_new)
    wgt_r = jnp.exp(lw_r - m_new)
    wgt_c = jnp.sum(jnp.where(eye, wgt_r, 0.0), axis=1, keepdims=True)
    wk = k * wgt_c
    c_new = decay * c_s[...] + _dot(wk.T.astype(BF16), v_bf)
    n_new = decay * n_s[...] + jnp.sum(wk, axis=0, keepdims=True)
    c_s[...] = c_new
    n_s[...] = n_new
    m_s[...] = jnp.broadcast_to(m_new, m_s.shape)

    @pl.when(c == pl.num_programs(1) - 1)
    def _final():
        c_out[0] = c_new
        n_out[0] = n_new
        m_out[0] = jnp.broadcast_to(m_new, (1, LANES))


def mlstm_prompt(z, li_rows, lf_rows, norm_g):
    T = z.shape[0]
    nc = T // ML_CHUNK
    kq = ML_W_QK // ML_DQK
    kv = 2 * ML_W_QK // ML_DV
    ko = kv + ML_HEADS
    return pl.pallas_call(
        _mlstm_prompt_body,
        out_shape=[jax.ShapeDtypeStruct((T, ML_W_V), BF16),
                   jax.ShapeDtypeStruct((ML_HEADS, ML_DQK, ML_DV), F32),
                   jax.ShapeDtypeStruct((ML_HEADS, 1, ML_DQK), F32),
                   jax.ShapeDtypeStruct((ML_HEADS, 1, LANES), F32)],
        grid=(ML_HEADS, nc),
        in_specs=[pl.BlockSpec((ML_CHUNK, ML_DQK), lambda h, c: (c, h)),
                  pl.BlockSpec((ML_CHUNK, ML_DQK), lambda h, c: (c, kq + h)),
                  pl.BlockSpec((ML_CHUNK, ML_DV), lambda h, c: (c, kv + h)),
                  pl.BlockSpec((ML_CHUNK, ML_DV), lambda h, c: (c, ko + h)),
                  pl.BlockSpec((1, 1, 1, ML_CHUNK), lambda h, c: (h, c, 0, 0)),
                  pl.BlockSpec((1, 1, 1, ML_CHUNK), lambda h, c: (h, c, 0, 0)),
                  pl.BlockSpec((1, ML_DV), lambda h, c: (0, h))],
        out_specs=[pl.BlockSpec((ML_CHUNK, ML_DV), lambda h, c: (c, h)),
                   pl.BlockSpec((1, ML_DQK, ML_DV), lambda h, c: (h, 0, 0)),
                   pl.BlockSpec((1, 1, ML_DQK), lambda h, c: (h, 0, 0)),
                   pl.BlockSpec((1, 1, LANES), lambda h, c: (h, 0, 0))],
        scratch_shapes=[pltpu.VMEM((ML_DQK, ML_DV), F32),
                        pltpu.VMEM((1, ML_DQK), F32),
                        pltpu.VMEM((1, LANES), F32)],
        compiler_params=_params("parallel", "arbitrary"),
        name="mlstm_prompt",
    )(z, z, z, z, li_rows, lf_rows, norm_g)


def _mlstm_decode_body(c_ref, n_ref, q_ref, k_ref, v_ref, o_ref, gate_ref, g_ref,
                       hm_ref, c_out, n_out, m_out):
    for h in range(ML_HEADS):
        C = c_ref[0, h]
        n_c = n_ref[0, h]
        q_c = q_ref[0, h]
        k_c = k_ref[0, h] * (ML_DQK ** -0.5)
        v_r = v_ref[0, h]
        gt = gate_ref[0, h]
        li, lf, m0 = gt[:, 0:1], gt[:, 1:2], gt[:, 2:3]
        inter = lf + m0
        m_new = jnp.maximum(inter, li)
        w_inter = jnp.exp(inter - m_new)
        w_in = jnp.exp(li - m_new)
        qk = jnp.sum(q_c * k_c, axis=0, keepdims=True) * w_in
        qc = jnp.sum(C * q_c, axis=0, keepdims=True)
        num = w_inter * qc + qk * v_r
        den = w_inter * jnp.sum(q_c * n_c, axis=0, keepdims=True) + qk
        hh = num / jnp.maximum(jnp.abs(den), jnp.exp(-m_new))
        hn = hh * lax.rsqrt(jnp.mean(hh * hh, axis=1, keepdims=True) + LN_EPS)
        hm_ref[0, h] = (hn * g_ref[h] * jax.nn.sigmoid(o_ref[0, h])).astype(hm_ref.dtype)
        wk = k_c * w_in
        c_out[0, h] = w_inter * C + wk * v_r
        n_out[0, h] = w_inter * n_c + wk
        m_out[0, h] = jnp.broadcast_to(m_new, (1, LANES))


def mlstm_decode(state_c, state_n_col, q_col, k_col, v_row, o_row, gates, norm_g):
    B = state_c.shape[0]
    H = ML_HEADS
    blk4 = lambda *s: pl.BlockSpec((1, H) + s, lambda b: (b, 0, 0, 0))
    return pl.pallas_call(
        _mlstm_decode_body,
        out_shape=[jax.ShapeDtypeStruct((B, H, 1, ML_DV), BF16),
                   jax.ShapeDtypeStruct((B, H, ML_DQK, ML_DV), F32),
                   jax.ShapeDtypeStruct((B, H, ML_DQK, 1), F32),
                   jax.ShapeDtypeStruct((B, H, 1, LANES), F32)],
        grid=(B,),
        in_specs=[blk4(ML_DQK, ML_DV), blk4(ML_DQK, 1), blk4(ML_DQK, 1), blk4(ML_DQK, 1),
                  blk4(1, ML_DV), blk4(1, ML_DV), blk4(1, LANES),
                  pl.BlockSpec((H, 1, ML_DV), lambda b: (0, 0, 0))],
        out_specs=[blk4(1, ML_DV), blk4(ML_DQK, ML_DV), blk4(ML_DQK, 1), blk4(1, LANES)],
        compiler_params=_params("parallel"),
        name="mlstm_decode",
    )(state_c, state_n_col, q_col, k_col, v_row, o_row, gates, norm_g)


def _fox_prompt_body(qi_tab, ki_tab, q_ref, k_ref, v_ref, kb_ref, kbq_ref, o_ref, m_s, l_s, acc_s):
    p = pl.program_id(1)
    qi = qi_tab[p]
    ki = ki_tab[p]
    tb = q_ref.shape[0]

    @pl.when(ki == 0)
    def _init():
        m_s[...] = jnp.full_like(m_s, -jnp.inf)
        l_s[...] = jnp.zeros_like(l_s)
        acc_s[...] = jnp.zeros_like(acc_s)

    kb = kb_ref[0] - kbq_ref[0][:, 0:1]
    t = _dot_nt(q_ref[...], k_ref[...]) + kb

    def update(t):
        m_prev = m_s[...]
        m_new = jnp.maximum(m_prev, jnp.max(t, axis=1, keepdims=True))
        alpha = jnp.exp(m_prev - m_new)
        pr = jnp.exp(t - m_new)
        l_s[...] = alpha * l_s[...] + jnp.sum(pr, axis=1, keepdims=True)
        acc_s[...] = alpha * acc_s[...] + _dot(pr.astype(BF16), v_ref[...])
        m_s[...] = m_new

    @pl.when(ki < qi)
    def _off_diag():
        update(t)

    @pl.when(ki == qi)
    def _diag():
        row = lax.broadcasted_iota(jnp.int32, (tb, tb), 0)
        col = lax.broadcasted_iota(jnp.int32, (tb, tb), 1)
        update(jnp.where(col <= row, t, -jnp.inf))
        o_ref[...] = (acc_s[...] / l_s[...]).astype(o_ref.dtype)


def fox_prompt(q, k, v, neg_cum, tb=FOX_BLOCK):
    T = q.shape[0]
    tb = min(tb, T)
    nb = T // tb
    pairs = [(i, j) for i in range(nb) for j in range(i + 1)]
    qi_tab = jnp.array([a for a, _ in pairs], jnp.int32)
    ki_tab = jnp.array([b for _, b in pairs], jnp.int32)
    grid_spec = pltpu.PrefetchScalarGridSpec(
        num_scalar_prefetch=2,
        grid=(FX_HEADS, len(pairs)),
        in_specs=[pl.BlockSpec((tb, FX_HD), lambda h, p, qt, kt: (qt[p], h)),
                  pl.BlockSpec((tb, FX_HD), lambda h, p, qt, kt: (kt[p], h)),
                  pl.BlockSpec((tb, FX_HD), lambda h, p, qt, kt: (kt[p], h)),
                  pl.BlockSpec((1, 1, tb), lambda h, p, qt, kt: (h, 0, kt[p])),
                  pl.BlockSpec((1, 1, tb), lambda h, p, qt, kt: (h, 0, qt[p]))],
        out_specs=pl.BlockSpec((tb, FX_HD), lambda h, p, qt, kt: (qt[p], h)),
        scratch_shapes=[pltpu.VMEM((tb, 1), F32), pltpu.VMEM((tb, 1), F32),
                        pltpu.VMEM((tb, FX_HD), F32)])
    return pl.pallas_call(
        _fox_prompt_body,
        out_shape=jax.ShapeDtypeStruct((T, FX_W), BF16),
        grid_spec=grid_spec,
        compiler_params=_params("parallel", "arbitrary"),
        name="fox_prompt",
    )(qi_tab, ki_tab, q, k, v, neg_cum, neg_cum)


def _fox_decode_body(pt_ref, q_ref, kc_ref, vc_ref, lft_ref, knew_ref, vnew_ref, lfnew_ref, later_ref,
                     o_ref, m_s, l_s, acc_s, carry_s):
    pg = pl.program_id(1)
    H = FX_HEADS
    q = q_ref[0]

    @pl.when(pg == 0)
    def _init():
        m_s[...] = jnp.sum(q.astype(F32) * knew_ref[0], axis=1, keepdims=True)
        l_s[...] = jnp.ones_like(l_s)
        acc_s[...] = vnew_ref[0]
        carry_s[...] = jnp.zeros_like(carry_s)

    lpt = lft_ref[0]
    hi, mid, lo = _split3(lpt)
    decay = _dot_nt(jnp.concatenate([hi, mid, lo], axis=0), later_ref[...])
    t = (_dot_nt(q, kc_ref[0].astype(BF16)) + decay[0:H] + decay[H:2 * H] + decay[2 * H:3 * H]
         + (carry_s[...] + lfnew_ref[0]))
    carry_s[...] = carry_s[...] + jnp.sum(lpt, axis=1, keepdims=True)
    sub = lax.broadcasted_iota(jnp.int32, t.shape, 0)
    lane = lax.broadcasted_iota(jnp.int32, t.shape, 1)
    t = jnp.where((lane & (H - 1)) == sub, t, -jnp.inf)
    m_prev = m_s[...]
    m_new = jnp.maximum(m_prev, jnp.max(t, axis=1, keepdims=True))
    alpha = jnp.exp(m_prev - m_new)
    pr = jnp.exp(t - m_new)
    l_s[...] = alpha * l_s[...] + jnp.sum(pr, axis=1, keepdims=True)
    acc_s[...] = alpha * acc_s[...] + _dot(pr.astype(BF16), vc_ref[0].astype(BF16))
    m_s[...] = m_new

    @pl.when(pg == pl.num_programs(1) - 1)
    def _final():
        o_ref[0] = acc_s[...] / l_s[...]


def fox_decode(page_table, q, cache_k, cache_v, logf_t, k_new, v_new, lf_new, later):
    B, n_pages = page_table.shape
    H, hd = FX_HEADS, FX_HD
    assert H & (H - 1) == 0
    rows = PAGE_SIZE * H
    page = lambda b, p, pt: (pt[b * n_pages + n_pages - 1 - p], 0, 0)
    seq = lambda b, p, pt: (b, 0, 0)
    grid_spec = pltpu.PrefetchScalarGridSpec(
        num_scalar_prefetch=1,
        grid=(B, n_pages),
        in_specs=[pl.BlockSpec((1, H, hd), seq),
                  pl.BlockSpec((1, rows, hd), page),
                  pl.BlockSpec((1, rows, hd), page),
                  pl.BlockSpec((1, H, PAGE_SIZE), page),
                  pl.BlockSpec((1, H, hd), seq),
                  pl.BlockSpec((1, H, hd), seq),
                  pl.BlockSpec((1, H, 1), seq),
                  pl.BlockSpec((rows, PAGE_SIZE), lambda b, p, pt: (0, 0))],
        out_specs=pl.BlockSpec((1, H, hd), seq),
        scratch_shapes=[pltpu.VMEM((H, 1), F32), pltpu.VMEM((H, 1), F32),
                        pltpu.VMEM((H, hd), F32), pltpu.VMEM((H, 1), F32)])
    return pl.pallas_call(
        _fox_decode_body,
        out_shape=jax.ShapeDtypeStruct((B, H, hd), F32),
        grid_spec=grid_spec,
        compiler_params=_params("parallel", "arbitrary"),
        name="fox_decode",
    )(page_table.reshape(-1), q, cache_k, cache_v, logf_t, k_new, v_new, lf_new, later)


def _first_argmax(x, n):
    io = lax.broadcasted_iota(jnp.int32, x.shape, 0)
    m = jnp.max(x, axis=0, keepdims=True)
    i = jnp.min(jnp.where(x == m, io, n), axis=0, keepdims=True)
    return m, i, io


def _stack_rows(rows, dtype):
    n = len(rows)
    io = lax.broadcasted_iota(jnp.int32, (n, rows[0].shape[1]), 0)
    out = jnp.broadcast_to(rows[0], io.shape).astype(dtype)
    for r in range(1, n):
        out = jnp.where(io == r, rows[r].astype(dtype), out)
    return out


def _route_body(h_ref, whi_ref, wlo_ref, b_ref, before_ref, idx_ref, w_ref, rank_ref, cnt_ref, carry_s):
    @pl.when(pl.program_id(0) == 0)
    def _init():
        carry_s[...] = jnp.zeros_like(carry_s)

    h = h_ref[...]
    h_hi = h.astype(BF16)
    h_lo = (h - h_hi.astype(F32)).astype(BF16)
    whi = whi_ref[...]
    logits = _dot_nt(whi, h_hi) + _dot_nt(whi, h_lo) + _dot_nt(wlo_ref[...], h_hi)
    s = jax.nn.sigmoid(logits)
    sb = s + b_ref[...]
    E = sb.shape[0]
    gsz = E // N_GROUPS
    neg = -jnp.inf

    gscore = []
    for g in range(N_GROUPS):
        x = sb[g * gsz:(g + 1) * gsz]
        m1, i1, io = _first_argmax(x, gsz)
        m2 = jnp.max(jnp.where(io == i1, neg, x), axis=0, keepdims=True)
        gscore.append(m1 + m2)
    cur = _stack_rows(gscore, F32)
    keep = jnp.zeros(cur.shape, F32)
    for _ in range(TOPK_GROUPS):
        _, ig, iog = _first_argmax(cur, N_GROUPS)
        hit = iog == ig
        keep = jnp.where(hit, 1.0, keep)
        cur = jnp.where(hit, neg, cur)
    blocks = []
    for g in range(N_GROUPS):
        keep_g = jnp.max(jnp.where(iog == g, keep, 0.0), axis=0, keepdims=True)
        blocks.append(jnp.where(keep_g > 0.0, sb[g * gsz:(g + 1) * gsz], neg))
    cand = jnp.concatenate(blocks, axis=0)

    sel = jnp.zeros(cand.shape, F32)
    idxs, ws = [], []
    for _ in range(TOP_K):
        _, ik, ioe = _first_argmax(cand, E)
        hit = ioe == ik
        idxs.append(ik)
        ws.append(jnp.sum(jnp.where(hit, s, 0.0), axis=0, keepdims=True))
        cand = jnp.where(hit, neg, cand)
        sel = jnp.where(hit, 1.0, sel)
    wsum = ws[0]
    for k in range(1, TOP_K):
        wsum = wsum + ws[k]
    ws = [w / wsum * ROUTED_SCALE for w in ws]

    prefix = _dot(sel.astype(BF16), before_ref[...]) + carry_s[...]
    ranks = [jnp.sum(jnp.where(ioe == ik, prefix, 0.0), axis=0, keepdims=True) for ik in idxs]
    carry_s[...] = carry_s[...] + jnp.sum(sel, axis=1, keepdims=True)
    cnt_ref[...] = carry_s[...]
    idx_ref[...] = _stack_rows(idxs, jnp.int32)
    w_ref[...] = _stack_rows(ws, F32)
    rank_ref[...] = _stack_rows(ranks, F32).astype(jnp.int32)


def route_tokens(h, w_hi_t, w_lo_t, b_col, tm=640):
    T, D = h.shape
    E = w_hi_t.shape[0]
    tm = _row_tile(T, tm)
    tok = lax.broadcasted_iota(jnp.int32, (tm, tm), 0)
    before = (tok < tok.T).astype(BF16)
    kt = lambda i: (0, i)
    const = lambda i: (0, 0)
    return pl.pallas_call(
        _route_body,
        out_shape=[jax.ShapeDtypeStruct((TOP_K, T), jnp.int32), jax.ShapeDtypeStruct((TOP_K, T), F32),
                   jax.ShapeDtypeStruct((TOP_K, T), jnp.int32), jax.ShapeDtypeStruct((E, 1), F32)],
        grid=(T // tm,),
        in_specs=[pl.BlockSpec((tm, D), lambda i: (i, 0)),
                  pl.BlockSpec((E, D), const), pl.BlockSpec((E, D), const),
                  pl.BlockSpec((E, 1), const), pl.BlockSpec((tm, tm), const)],
        out_specs=[pl.BlockSpec((TOP_K, tm), kt), pl.BlockSpec((TOP_K, tm), kt),
                   pl.BlockSpec((TOP_K, tm), kt), pl.BlockSpec((E, 1), const)],
        scratch_shapes=[pltpu.VMEM((E, 1), F32)],
        compiler_params=_params("arbitrary"),
        name="route_tokens",
    )(h, w_hi_t, w_lo_t, b_col, before)


def _slot(ps_ref, idx_ref, rank_ref, k, t):
    return ps_ref[idx_ref[k, t]] + rank_ref[k, t]


def _dispatch_body(ps_ref, idx_ref, rank_ref, h_ref, xs_in_ref, xs_ref, sem):
    del xs_in_ref
    tq = h_ref.shape[0]

    def row_copy(t, k):
        return pltpu.make_async_copy(h_ref.at[pl.ds(t, 1)],
                                     xs_ref.at[pl.ds(_slot(ps_ref, idx_ref, rank_ref, k, t), 1)], sem)

    def start(t, c):
        for k in range(TOP_K):
            row_copy(t, k).start()
        return c

    def wait(t, c):
        for k in range(TOP_K):
            row_copy(t, k).wait()
        return c

    lax.fori_loop(0, tq, start, 0)
    lax.fori_loop(0, tq, wait, 0)


def moe_dispatch(pad_start, idx, rank, h_packed, n_rows, tq=128):
    T, Dp = h_packed.shape
    tq = _row_tile(T, tq)
    smem = lambda: pl.BlockSpec((TOP_K, tq), lambda i, ps: (0, i), memory_space=pltpu.SMEM)
    grid_spec = pltpu.PrefetchScalarGridSpec(
        num_scalar_prefetch=1,
        grid=(T // tq,),
        in_specs=[smem(), smem(),
                  pl.BlockSpec((tq, Dp), lambda i, ps: (i, 0)),
                  pl.BlockSpec(memory_space=pl.ANY)],
        out_specs=pl.BlockSpec(memory_space=pl.ANY),
        scratch_shapes=[pltpu.SemaphoreType.DMA(())])
    return pl.pallas_call(
        _dispatch_body,
        out_shape=jax.ShapeDtypeStruct((n_rows, Dp), h_packed.dtype),
        grid_spec=grid_spec,
        input_output_aliases={4: 0},
        compiler_params=_params("arbitrary"),
        name="moe_dispatch",
    )(pad_start, idx, rank, h_packed, jnp.zeros((n_rows, Dp), h_packed.dtype))


def _combine_body(ps_ref, idx_ref, rank_ref, w_ref, yb_ref, o_ref, buf, sem):
    tq = o_ref.shape[0]

    def row_copy(t, k):
        return pltpu.make_async_copy(yb_ref.at[pl.ds(_slot(ps_ref, idx_ref, rank_ref, k, t), 1)],
                                     buf.at[k, pl.ds(t, 1)], sem)

    def start(t, c):
        for k in range(TOP_K):
            row_copy(t, k).start()
        return c

    def wait(t, c):
        for k in range(TOP_K):
            row_copy(t, k).wait()
        return c

    lax.fori_loop(0, tq, start, 0)
    lax.fori_loop(0, tq, wait, 0)
    acc = w_ref[:, 0:1] * buf[0]
    for k in range(1, TOP_K):
        acc = acc + w_ref[:, k:k + 1] * buf[k]
    o_ref[...] = acc


def moe_combine(pad_start, idx, rank, w_col, yb, tq=128):
    T = w_col.shape[0]
    D = yb.shape[1]
    tq = _row_tile(T, tq)
    smem = lambda: pl.BlockSpec((TOP_K, tq), lambda i, ps: (0, i), memory_space=pltpu.SMEM)
    grid_spec = pltpu.PrefetchScalarGridSpec(
        num_scalar_prefetch=1,
        grid=(T // tq,),
        in_specs=[smem(), smem(),
                  pl.BlockSpec((tq, TOP_K), lambda i, ps: (i, 0)),
                  pl.BlockSpec(memory_space=pl.ANY)],
        out_specs=pl.BlockSpec((tq, D), lambda i, ps: (i, 0)),
        scratch_shapes=[pltpu.VMEM((TOP_K, tq, D), F32), pltpu.SemaphoreType.DMA(())])
    return pl.pallas_call(
        _combine_body,
        out_shape=jax.ShapeDtypeStruct((T, D), F32),
        grid_spec=grid_spec,
        compiler_params=_params("arbitrary"),
        name="moe_combine",
    )(pad_start, idx, rank, w_col, yb)


def _moe_body(te_ref, tf_ref, nv_ref, x_ref, wg_ref, wu_ref, wd_ref, y_ref, wg_s, wu_s, wd_s):
    i = pl.program_id(0)

    @pl.when(i >= nv_ref[0])
    def _unused_tile():
        y_ref[...] = jnp.zeros_like(y_ref)

    @pl.when(jnp.logical_and(i < nv_ref[0], tf_ref[i] == 1))
    def _load_expert():
        wg_s[...] = wg_ref[0].astype(BF16)
        wu_s[...] = wu_ref[0].astype(BF16)
        wd_s[...] = wd_ref[0].astype(BF16)

    @pl.when(i < nv_ref[0])
    def _tile():
        xw = x_ref[...]
        half = xw.shape[1]
        x_lo = lax.bitcast_convert_type(xw << 16, F32).astype(BF16)
        x_hi = lax.bitcast_convert_type(xw & jnp.int32(-65536), F32).astype(BF16)
        g = _dot(x_lo, wg_s[0:half]) + _dot(x_hi, wg_s[half:2 * half])
        u = _dot(x_lo, wu_s[0:half]) + _dot(x_hi, wu_s[half:2 * half])
        y_ref[...] = _dot((jax.nn.silu(g) * u).astype(BF16), wd_s[...])


def moe_experts(xs, tile_expert, tile_first, n_valid, w_gate, w_up, w_down):
    R, Dp = xs.shape
    D = 2 * Dp
    tm = MOE_TILE
    nt = R // tm
    De = w_gate.shape[2]
    grid_spec = pltpu.PrefetchScalarGridSpec(
        num_scalar_prefetch=3,
        grid=(nt,),
        in_specs=[pl.BlockSpec((tm, Dp), lambda i, te, tf, nv: (i, 0)),
                  pl.BlockSpec((1, D, De), lambda i, te, tf, nv: (te[i], 0, 0)),
                  pl.BlockSpec((1, D, De), lambda i, te, tf, nv: (te[i], 0, 0)),
                  pl.BlockSpec((1, De, D), lambda i, te, tf, nv: (te[i], 0, 0))],
        out_specs=pl.BlockSpec((tm, D), lambda i, te, tf, nv: (i, 0)),
        scratch_shapes=[pltpu.VMEM((D, De), BF16), pltpu.VMEM((D, De), BF16),
                        pltpu.VMEM((De, D), BF16)])
    return pl.pallas_call(
        _moe_body,
        out_shape=jax.ShapeDtypeStruct((R, D), F32),
        grid_spec=grid_spec,
        compiler_params=_params("arbitrary"),
        name="moe_experts",
    )(tile_expert, tile_first, n_valid, xs, w_gate, w_up, w_down)


def routed_moe(h_f32, h_bf, w_r_hi_t, w_r_lo_t, b_router, w_gate, w_up, w_down):
    T, D = h_f32.shape
    E = N_EXPERTS
    tm = MOE_TILE
    idx, w, rank, cnt = route_tokens(h_f32, w_r_hi_t, w_r_lo_t, b_router.astype(F32).reshape(E, 1))

    counts = cnt[:, 0].astype(jnp.int32)
    padded = (counts + tm - 1) // tm * tm
    pad_end = jnp.cumsum(padded)
    pad_start = pad_end - padded
    nt = -(-(T * TOP_K) // tm) + E
    n_valid = (pad_end[-1] // tm).astype(jnp.int32)
    tile_row = jnp.minimum(jnp.arange(nt, dtype=jnp.int32), n_valid - 1)
    tile_expert = jnp.minimum(jnp.searchsorted(pad_end, tile_row * tm, side='right'), E - 1).astype(jnp.int32)
    tile_first = jnp.concatenate([jnp.ones((1,), jnp.int32),
                                  (tile_expert[1:] != tile_expert[:-1]).astype(jnp.int32)])

    half = D // 2
    bits = lax.bitcast_convert_type(h_bf, jnp.uint16).astype(jnp.uint32)
    h_packed = lax.bitcast_convert_type(bits[:, :half] | (bits[:, half:] << 16), jnp.int32)
    xs = moe_dispatch(pad_start, idx, rank, h_packed, nt * tm)
    yb = moe_experts(xs, tile_expert, tile_first, n_valid.reshape(1), w_gate, w_up, w_down)
    return moe_combine(pad_start, idx, rank, w.T, yb)


def _prep_weights(w_in, b_ml_i, b_ml_f, b_fx_f, w_br_ml, w_br_fx, w_out, w_router,
                  w_sh_gate, w_sh_up, w_sh_down, w_ple_gate, w_ple_proj):
    o_ml = 2 * ML_W_QK + 2 * ML_W_V
    o_fx = o_ml + 2 * ML_HEADS
    o_ff = o_fx + 3 * FX_W
    o_g = o_ff + FX_HEADS
    wb = w_in.astype(BF16)
    small = jnp.concatenate([wb[:, o_ml:o_fx], wb[:, o_ff:o_g]], axis=1)
    small = jnp.pad(small, ((0, 0), (0, LANES - small.shape[1])))
    w_r_hi = w_router.T.astype(BF16)
    w_r_lo = (w_router.T - w_r_hi.astype(F32)).astype(BF16)
    return dict(
        w_ml=wb[:, :o_ml], w_fq=wb[:, o_fx:o_fx + FX_W], w_fk=wb[:, o_fx + FX_W:o_fx + 2 * FX_W],
        w_fv=wb[:, o_fx + 2 * FX_W:o_ff], w_g=wb[:, o_g:], w_small=small,
        w_br_ml=w_br_ml.astype(BF16), w_br_fx=w_br_fx.astype(BF16), w_out=w_out.astype(BF16),
        w_r_hi=w_r_hi, w_r_lo=w_r_lo,
        w_sh_gate=w_sh_gate.astype(BF16), w_sh_up=w_sh_up.astype(BF16), w_sh_down=w_sh_down.astype(BF16),
        w_ple_gate=w_ple_gate.astype(BF16), w_ple_proj=w_ple_proj.astype(BF16))


def _in_proj(x_bf, W):
    (z_ml,) = matmul(x_bf, W['w_ml'], name="proj_ml")
    (fq,) = matmul(x_bf, W['w_fq'], out_dtypes=(BF16,), scale=FX_SCALE, name="proj_fq")
    fk, fk_bf = matmul(x_bf, W['w_fk'], out_dtypes=(F32, BF16), name="proj_fk")
    fv, fv_bf = matmul(x_bf, W['w_fv'], out_dtypes=(F32, BF16), name="proj_fv")
    (z_g,) = matmul(x_bf, W['w_g'], name="proj_gates")
    (z_s,) = matmul(x_bf, W['w_small'], name="proj_small")
    return z_ml, fq, fk, fk_bf, fv, fv_bf, z_g, z_s


def _small_gates(z_s, b_ml_i, b_ml_f, b_fx_f):
    H = ML_HEADS
    li = z_s[:, 0:H] + b_ml_i
    lf = jax.nn.log_sigmoid(z_s[:, H:2 * H] + b_ml_f)
    lf_fx = jax.nn.log_sigmoid(z_s[:, 2 * H:2 * H + FX_HEADS] + b_fx_f)
    return li, lf, lf_fx


def kernel(x_prompt, x_sample, p_prompt, p_sample, cache_k, cache_v, cache_logf, state_C, state_n, state_m,
           page_table, w_in, b_ml_i, b_ml_f, b_fx_f, ml_norm_g, w_br_ml, w_br_fx, w_out, ln1_g, ln1_b,
           w_router, b_router, w_exp_gate, w_exp_up, w_exp_down, w_sh_gate, w_sh_up, w_sh_down,
           w_ple_gate, w_ple_proj, ln2_g, ln2_b):
    T = x_prompt.shape[1]
    B = x_sample.shape[0]
    H = ML_HEADS
    W = _prep_weights(w_in[0], b_ml_i[0], b_ml_f[0], b_fx_f[0], w_br_ml[0], w_br_fx[0], w_out[0], w_router[0],
                      w_sh_gate[0], w_sh_up[0], w_sh_down[0], w_ple_gate[0], w_ple_proj[0])
    xp = x_prompt[0]
    xs = x_sample[:, 0]

    z_ml, fq, fk, fk_bf, fv, fv_bf, z_g, z_s = _in_proj(xp.astype(BF16), W)
    li, lf, lf_fx = _small_gates(z_s, b_ml_i[0], b_ml_f[0], b_fx_f[0])
    nc = T // ML_CHUNK
    to_rows = lambda a: a.T.reshape(H, nc, 1, ML_CHUNK)
    hm_p, C_p, n_p, m_p = mlstm_prompt(z_ml, to_rows(li), to_rows(lf), ml_norm_g)
    neg_cum = (-jnp.cumsum(lf_fx, axis=0)).T.reshape(FX_HEADS, 1, T)
    hfx_p = fox_prompt(fq, fk_bf, fv_bf, neg_cum)
    merged_p = gated_merge(hm_p, W['w_br_ml'], hfx_p, W['w_br_fx'], z_g, BF16)
    h1_p, h1b_p = out_proj_ln(merged_p, W['w_out'], xp, ln1_g, ln1_b)

    zd_ml, dq, dk, dk_bf, dv, dv_bf, zd_g, zd_s = _in_proj(xs.astype(BF16), W)
    dli, dlf, dlf_fx = _small_gates(zd_s, b_ml_i[0], b_ml_f[0], b_fx_f[0])
    col = lambda a: a.reshape(B, H, ML_DQK, 1)
    gates = jnp.stack([dli, dlf, state_m[0]], axis=-1)
    gates = jnp.pad(gates, ((0, 0), (0, 0), (0, LANES - 3))).reshape(B, H, 1, LANES)
    hm_d, C_d, n_d, m_d = mlstm_decode(
        state_C[0], col(state_n[0]), col(zd_ml[:, :ML_W_QK]), col(zd_ml[:, ML_W_QK:2 * ML_W_QK]),
        zd_ml[:, 2 * ML_W_QK:2 * ML_W_QK + ML_W_V].reshape(B, H, 1, ML_DV),
        zd_ml[:, 2 * ML_W_QK + ML_W_V:].reshape(B, H, 1, ML_DV),
        gates, ml_norm_g.reshape(H, 1, ML_DV))
    n_pool = cache_k.shape[1]
    page_rows = PAGE_SIZE * FX_HEADS
    row_of = jnp.arange(page_rows, dtype=jnp.int32) // FX_HEADS
    later = (jnp.arange(PAGE_SIZE, dtype=jnp.int32)[None, :] > row_of[:, None]).astype(BF16)
    heads = lambda a: a.reshape(B, FX_HEADS, FX_HD)
    hfx_d = fox_decode(page_table, heads(dq), cache_k.reshape(n_pool, page_rows, FX_HD),
                       cache_v.reshape(n_pool, page_rows, FX_HD), jnp.swapaxes(cache_logf[0], 1, 2),
                       heads(dk), heads(dv), dlf_fx[:, :, None], later)
    merged_d = gated_merge(hm_d.reshape(B, ML_W_V), W['w_br_ml'], hfx_d.reshape(B, FX_W).astype(BF16),
                           W['w_br_fx'], zd_g, BF16)
    h1_d, h1b_d = out_proj_ln(merged_d, W['w_out'], xs, ln1_g, ln1_b)

    h1_all = jnp.concatenate([h1_p, h1_d], axis=0)
    h1b_all = jnp.concatenate([h1b_p, h1b_d], axis=0)
    moe = routed_moe(h1_all, h1b_all, W['w_r_hi'], W['w_r_lo'], b_router[0],
                     w_exp_gate[0], w_exp_up[0], w_exp_down[0])
    ple_p = ple_embed(h1b_p, W['w_ple_gate'], p_prompt[0, 0].astype(BF16), W['w_ple_proj'])
    ple_d = ple_embed(h1b_d, W['w_ple_gate'], p_sample[0, :, 0].astype(BF16), W['w_ple_proj'])
    y_p = ffn_tail(h1b_p, h1_p, moe[:T], ple_p, W['w_sh_gate'], W['w_sh_up'], W['w_sh_down'], ln2_g, ln2_b)
    y_d = ffn_tail(h1b_d, h1_d, moe[T:], ple_d, W['w_sh_gate'], W['w_sh_up'], W['w_sh_down'], ln2_g, ln2_b)

    return (y_p[None], y_d[:, None],
            fk.reshape(1, 1, T, FX_HEADS, FX_HD), fv.reshape(1, 1, T, FX_HEADS, FX_HD), lf_fx[None, None],
            C_p[None, None], n_p.reshape(1, 1, H, ML_DQK), m_p[:, 0, 0].reshape(1, 1, H),
            dk.reshape(1, B, 1, FX_HEADS, FX_HD), dv.reshape(1, B, 1, FX_HEADS, FX_HD), dlf_fx[None, :, None],
            C_d[None], n_d.reshape(1, B, H, ML_DQK), m_d[:, :, 0, 0][None])
```

```python
import functools

import jax
import jax.numpy as jnp
from jax import lax
from jax.experimental import pallas as pl
from jax.experimental.pallas import tpu as pltpu

F32 = jnp.float32
BF16 = jnp.bfloat16

D_MODEL = 2048
ML_HEADS = 4
ML_DQK = 256
ML_DV = 512
ML_CHUNK = 128
ML_W_QK = ML_HEADS * ML_DQK
ML_W_V = ML_HEADS * ML_DV
FX_HEADS = 16
FX_HD = 128
FX_W = FX_HEADS * FX_HD
FX_SCALE = FX_HD ** -0.5
PAGE_SIZE = 128
N_EXPERTS = 256
TOP_K = 8
N_GROUPS = 8
TOPK_GROUPS = 4
D_EXPERT = 512
ROUTED_SCALE = 2.5
LN_EPS = 1e-5
DEEPNORM_ALPHA = 2.0 ** 0.25

LANES = 128
SUBLANES = 8
VMEM_LIMIT = 56 * 1024 * 1024
MOE_TILE = 128
FOX_BLOCK = 1024
FOX_DECODE_PAGES = (4, 2, 1)


def _params(*sem):
    return pltpu.CompilerParams(dimension_semantics=sem, vmem_limit_bytes=VMEM_LIMIT)


def _row_tile(m, target):
    if m <= target:
        return m
    best = max(t for t in range(LANES, target + 1, LANES) if m % t == 0)
    assert m % best == 0
    return best


def _dot(a, b):
    return jnp.dot(a, b, preferred_element_type=F32)


def _dot_nt(a, b):
    return lax.dot_general(a, b, (((1,), (1,)), ((), ())), preferred_element_type=F32)


def _split3(x):
    hi = x.astype(BF16)
    r1 = x - hi.astype(F32)
    mid = r1.astype(BF16)
    lo = (r1 - mid.astype(F32)).astype(BF16)
    return hi, mid, lo


def _layer_norm(y, g, b):
    mu = jnp.mean(y, axis=-1, keepdims=True)
    d = y - mu
    var = jnp.mean(d * d, axis=-1, keepdims=True)
    return d * lax.rsqrt(var + LN_EPS) * g + b


def _mm_body(a_ref, b_ref, *o_refs, scale):
    acc = _dot(a_ref[...], b_ref[...])
    if scale != 1.0:
        acc = acc * scale
    for o in o_refs:
        o[...] = acc.astype(o.dtype)


def matmul(a, b, out_dtypes=(F32,), tm=512, tn=512, scale=1.0, name="matmul"):
    M, K = a.shape
    N = b.shape[1]
    tm, tn = _row_tile(M, tm), _row_tile(N, tn)
    outs = pl.pallas_call(
        functools.partial(_mm_body, scale=scale),
        out_shape=[jax.ShapeDtypeStruct((M, N), d) for d in out_dtypes],
        grid=(M // tm, N // tn),
        in_specs=[pl.BlockSpec((tm, K), lambda i, j: (i, 0)),
                  pl.BlockSpec((K, tn), lambda i, j: (0, j))],
        out_specs=[pl.BlockSpec((tm, tn), lambda i, j: (i, j)) for _ in out_dtypes],
        compiler_params=_params("parallel", "parallel"),
        name=name,
    )(a, b)
    return outs


def _pair_body(a1_ref, b1_ref, a2_ref, b2_ref, g1_ref, g2_ref, o_ref, *, mode):
    y1 = _dot(a1_ref[...], b1_ref[...])
    y2 = _dot(a2_ref[...], b2_ref[...])
    if mode == "merge":
        out = jax.nn.sigmoid(g1_ref[...]) * y1 + jax.nn.sigmoid(g2_ref[...]) * y2
    else:
        out = jax.nn.sigmoid(y1) * y2
    o_ref[...] = out.astype(o_ref.dtype)


def gated_merge(a1, b1, a2, b2, gates, out_dtype, tm=512, tn=512):
    M, K = a1.shape
    N = b1.shape[1]
    tm, tn = _row_tile(M, tm), _row_tile(N, tn)
    nj = N // tn
    return pl.pallas_call(
        functools.partial(_pair_body, mode="merge"),
        out_shape=jax.ShapeDtypeStruct((M, N), out_dtype),
        grid=(M // tm, nj),
        in_specs=[pl.BlockSpec((tm, K), lambda i, j: (i, 0)),
                  pl.BlockSpec((K, tn), lambda i, j: (0, j)),
                  pl.BlockSpec((tm, K), lambda i, j: (i, 0)),
                  pl.BlockSpec((K, tn), lambda i, j: (0, j)),
                  pl.BlockSpec((tm, tn), lambda i, j: (i, j)),
                  pl.BlockSpec((tm, tn), lambda i, j: (i, j + nj))],
        out_specs=pl.BlockSpec((tm, tn), lambda i, j: (i, j)),
        compiler_params=_params("parallel", "parallel"),
        name="gated_merge",
    )(a1, b1, a2, b2, gates, gates)


def _ple_body(a1_ref, b1_ref, a2_ref, b2_ref, o_ref):
    y1 = _dot(a1_ref[...], b1_ref[...])
    y2 = _dot(a2_ref[...], b2_ref[...])
    o_ref[...] = (jax.nn.sigmoid(y1) * y2).astype(o_ref.dtype)


def ple_embed(h, w_gate, p, w_proj, tm=512, tn=512):
    M, K1 = h.shape
    K2 = p.shape[1]
    N = w_gate.shape[1]
    tm, tn = _row_tile(M, tm), _row_tile(N, tn)
    return pl.pallas_call(
        _ple_body,
        out_shape=jax.ShapeDtypeStruct((M, N), F32),
        grid=(M // tm, N // tn),
        in_specs=[pl.BlockSpec((tm, K1), lambda i, j: (i, 0)),
                  pl.BlockSpec((K1, tn), lambda i, j: (0, j)),
                  pl.BlockSpec((tm, K2), lambda i, j: (i, 0)),
                  pl.BlockSpec((K2, tn), lambda i, j: (0, j))],
        out_specs=pl.BlockSpec((tm, tn), lambda i, j: (i, j)),
        compiler_params=_params("parallel", "parallel"),
        name="ple_embed",
    )(h, w_gate, p, w_proj)


def _outln_body(mg_ref, w_ref, x_ref, g_ref, b_ref, hf_ref, hb_ref):
    y = DEEPNORM_ALPHA * x_ref[...] + _dot(mg_ref[...], w_ref[...])
    h = _layer_norm(y, g_ref[...], b_ref[...])
    hf_ref[...] = h
    hb_ref[...] = h.astype(BF16)


def out_proj_ln(merged, w_out, x, g, b, tm=256):
    M, K = merged.shape
    N = w_out.shape[1]
    tm = _row_tile(M, tm)
    return pl.pallas_call(
        _outln_body,
        out_shape=[jax.ShapeDtypeStruct((M, N), F32), jax.ShapeDtypeStruct((M, N), BF16)],
        grid=(M // tm,),
        in_specs=[pl.BlockSpec((tm, K), lambda i: (i, 0)),
                  pl.BlockSpec((K, N), lambda i: (0, 0)),
                  pl.BlockSpec((tm, N), lambda i: (i, 0)),
                  pl.BlockSpec((1, N), lambda i: (0, 0)),
                  pl.BlockSpec((1, N), lambda i: (0, 0))],
        out_specs=[pl.BlockSpec((tm, N), lambda i: (i, 0)),
                   pl.BlockSpec((tm, N), lambda i: (i, 0))],
        compiler_params=_params("parallel"),
        name="out_proj_ln",
    )(merged, w_out, x, g, b)


def _ffn_tail_body(hb_ref, hf_ref, moe_ref, ple_ref, wg_ref, wu_ref, wd_ref, g_ref, b_ref, y_ref):
    hb = hb_ref[...]
    sg = _dot(hb, wg_ref[...])
    su = _dot(hb, wu_ref[...])
    sh = _dot((jax.nn.silu(sg) * su).astype(BF16), wd_ref[...])
    y = DEEPNORM_ALPHA * hf_ref[...] + moe_ref[...] + sh + ple_ref[...]
    y_ref[...] = _layer_norm(y, g_ref[...], b_ref[...])


def ffn_tail(h_bf, h_f32, moe, ple, wg, wu, wd, g, b, tm=256):
    M, D = h_bf.shape
    Ds = wg.shape[1]
    tm = _row_tile(M, tm)
    row = lambda i: (i, 0)
    const = lambda i: (0, 0)
    return pl.pallas_call(
        _ffn_tail_body,
        out_shape=jax.ShapeDtypeStruct((M, D), F32),
        grid=(M // tm,),
        in_specs=[pl.BlockSpec((tm, D), row), pl.BlockSpec((tm, D), row),
                  pl.BlockSpec((tm, D), row), pl.BlockSpec((tm, D), row),
                  pl.BlockSpec((D, Ds), const), pl.BlockSpec((D, Ds), const),
                  pl.BlockSpec((Ds, D), const),
                  pl.BlockSpec((1, D), const), pl.BlockSpec((1, D), const)],
        out_specs=pl.BlockSpec((tm, D), row),
        compiler_params=_params("parallel"),
        name="ffn_tail",
    )(h_bf, h_f32, moe, ple, wg, wu, wd, g, b)


def _mlstm_prompt_body(q_ref, k_ref, v_ref, o_ref, li_ref, lf_ref, g_ref,
                       hm_ref, c_out, n_out, m_out, c_s, n_s, m_s):
    c = pl.program_id(1)
    L = ML_CHUNK

    @pl.when(c == 0)
    def _init():
        c_s[...] = jnp.zeros_like(c_s)
        n_s[...] = jnp.zeros_like(n_s)
        m_s[...] = jnp.zeros_like(m_s)

    q = q_ref[...]
    k = k_ref[...] * (ML_DQK ** -0.5)
    v_bf = v_ref[...].astype(BF16)
    q_bf = q.astype(BF16)
    li_r = li_ref[0, 0]
    lf_r = lf_ref[0, 0]

    row = lax.broadcasted_iota(jnp.int32, (L, L), 0)
    col = lax.broadcasted_iota(jnp.int32, (L, L), 1)
    eye = row == col
    causal = col <= row
    lf_c = jnp.sum(jnp.where(eye, lf_r, 0.0), axis=1, keepdims=True)
    b_c = jnp.sum(jnp.where(causal, lf_r, 0.0), axis=1, keepdims=True)
    b_r = jnp.sum(jnp.where(row <= col, lf_c, 0.0), axis=0, keepdims=True)
    b_tot = jnp.sum(lf_r, axis=1, keepdims=True)

    m_prev = m_s[:, 0:1]
    dlog = jnp.where(causal, b_c - b_r + li_r, -jnp.inf)
    inter = b_c + m_prev
    m_row = jnp.maximum(inter, jnp.max(dlog, axis=1, keepdims=True))
    w_inter = jnp.exp(inter - m_row)
    s = _dot_nt(q_bf, k.astype(BF16)) * jnp.exp(dlog - m_row)
    num = w_inter * _dot(q_bf, c_s[...].astype(BF16)) + _dot(s.astype(BF16), v_bf)
    den = w_inter * jnp.sum(q * n_s[...], axis=1, keepdims=True) + jnp.sum(s, axis=1, keepdims=True)
    h = num / jnp.maximum(jnp.abs(den), jnp.exp(-m_row))

    hn = h * lax.rsqrt(jnp.mean(h * h, axis=1, keepdims=True) + LN_EPS)
    hm_ref[...] = (hn * g_ref[...] * jax.nn.sigmoid(o_ref[...])).astype(hm_ref.dtype)

    lw_r = b_tot - b_r + li_r
    m_new = jnp.maximum(b_tot + m_prev, jnp.max(lw_r, axis=1, keepdims=True))
    decay = jnp.exp(b_tot + m_prev - m_new)
    wgt_r = jnp.exp(lw_r - m_new)
    wgt_c = jnp.sum(jnp.where(eye, wgt_r, 0.0), axis=1, keepdims=True)
    wk = k * wgt_c
    c_new = decay * c_s[...] + _dot(wk.T.astype(BF16), v_bf)
    n_new = decay * n_s[...] + jnp.sum(wk, axis=0, keepdims=True)
    c_s[...] = c_new
    n_s[...] = n_new
    m_s[...] = jnp.broadcast_to(m_new, m_s.shape)

    @pl.when(c == pl.num_programs(1) - 1)
    def _final():
        c_out[0] = c_new
        n_out[0] = n_new
        m_out[0] = jnp.broadcast_to(m_new, (1, LANES))


def mlstm_prompt(z, li_rows, lf_rows, norm_g):
    T = z.shape[0]
    nc = T // ML_CHUNK
    kq = ML_W_QK // ML_DQK
    kv = 2 * ML_W_QK // ML_DV
    ko = kv + ML_HEADS
    return pl.pallas_call(
        _mlstm_prompt_body,
        out_shape=[jax.ShapeDtypeStruct((T, ML_W_V), BF16),
                   jax.ShapeDtypeStruct((ML_HEADS, ML_DQK, ML_DV), F32),
                   jax.ShapeDtypeStruct((ML_HEADS, 1, ML_DQK), F32),
                   jax.ShapeDtypeStruct((ML_HEADS, 1, LANES), F32)],
        grid=(ML_HEADS, nc),
        in_specs=[pl.BlockSpec((ML_CHUNK, ML_DQK), lambda h, c: (c, h)),
                  pl.BlockSpec((ML_CHUNK, ML_DQK), lambda h, c: (c, kq + h)),
                  pl.BlockSpec((ML_CHUNK, ML_DV), lambda h, c: (c, kv + h)),
                  pl.BlockSpec((ML_CHUNK, ML_DV), lambda h, c: (c, ko + h)),
                  pl.BlockSpec((1, 1, 1, ML_CHUNK), lambda h, c: (h, c, 0, 0)),
                  pl.BlockSpec((1, 1, 1, ML_CHUNK), lambda h, c: (h, c, 0, 0)),
                  pl.BlockSpec((1, ML_DV), lambda h, c: (0, h))],
        out_specs=[pl.BlockSpec((ML_CHUNK, ML_DV), lambda h, c: (c, h)),
                   pl.BlockSpec((1, ML_DQK, ML_DV), lambda h, c: (h, 0, 0)),
                   pl.BlockSpec((1, 1, ML_DQK), lambda h, c: (h, 0, 0)),
                   pl.BlockSpec((1, 1, LANES), lambda h, c: (h, 0, 0))],
        scratch_shapes=[pltpu.VMEM((ML_DQK, ML_DV), F32),
                        pltpu.VMEM((1, ML_DQK), F32),
                        pltpu.VMEM((1, LANES), F32)],
        compiler_params=_params("parallel", "arbitrary"),
        name="mlstm_prompt",
    )(z, z, z, z, li_rows, lf_rows, norm_g)


def _mlstm_decode_body(c_ref, n_ref, q_ref, k_ref, v_ref, o_ref, gate_ref, g_ref,
                       hm_ref, c_out, n_out, m_out):
    for h in range(ML_HEADS):
        C = c_ref[0, h]
        n_c = n_ref[0, h]
        q_c = q_ref[0, h]
        k_c = k_ref[0, h] * (ML_DQK ** -0.5)
        v_r = v_ref[0, h]
        gt = gate_ref[0, h]
        li, lf, m0 = gt[:, 0:1], gt[:, 1:2], gt[:, 2:3]
        inter = lf + m0
        m_new = jnp.maximum(inter, li)
        w_inter = jnp.exp(inter - m_new)
        w_in = jnp.exp(li - m_new)
        qk = jnp.sum(q_c * k_c, axis=0, keepdims=True) * w_in
        qc = jnp.sum(C * q_c, axis=0, keepdims=True)
        num = w_inter * qc + qk * v_r
        den = w_inter * jnp.sum(q_c * n_c, axis=0, keepdims=True) + qk
        hh = num / jnp.maximum(jnp.abs(den), jnp.exp(-m_new))
        hn = hh * lax.rsqrt(jnp.mean(hh * hh, axis=1, keepdims=True) + LN_EPS)
        hm_ref[0, h] = (hn * g_ref[h] * jax.nn.sigmoid(o_ref[0, h])).astype(hm_ref.dtype)
        wk = k_c * w_in
        c_out[0, h] = w_inter * C + wk * v_r
        n_out[0, h] = w_inter * n_c + wk
        m_out[0, h] = jnp.broadcast_to(m_new, (1, LANES))


def mlstm_decode(state_c, state_n_col, q_col, k_col, v_row, o_row, gates, norm_g):
    B = state_c.shape[0]
    H = ML_HEADS
    blk4 = lambda *s: pl.BlockSpec((1, H) + s, lambda b: (b, 0, 0, 0))
    return pl.pallas_call(
        _mlstm_decode_body,
        out_shape=[jax.ShapeDtypeStruct((B, H, 1, ML_DV), BF16),
                   jax.ShapeDtypeStruct((B, H, ML_DQK, ML_DV), F32),
                   jax.ShapeDtypeStruct((B, H, ML_DQK, 1), F32),
                   jax.ShapeDtypeStruct((B, H, 1, LANES), F32)],
        grid=(B,),
        in_specs=[blk4(ML_DQK, ML_DV), blk4(ML_DQK, 1), blk4(ML_DQK, 1), blk4(ML_DQK, 1),
                  blk4(1, ML_DV), blk4(1, ML_DV), blk4(1, LANES),
                  pl.BlockSpec((H, 1, ML_DV), lambda b: (0, 0, 0))],
        out_specs=[blk4(1, ML_DV), blk4(ML_DQK, ML_DV), blk4(ML_DQK, 1), blk4(1, LANES)],
        compiler_params=_params("parallel"),
        name="mlstm_decode",
    )(state_c, state_n_col, q_col, k_col, v_row, o_row, gates, norm_g)


def _fox_prompt_body(qi_tab, ki_tab, q_ref, k_ref, v_ref, kb_ref, ones_ref, o_ref, m_s, acc_s):
    p = pl.program_id(1)
    qi = qi_tab[p]
    ki = ki_tab[p]
    tb = q_ref.shape[0]
    hd = q_ref.shape[1]

    @pl.when(ki == 0)
    def _init():
        m_s[...] = jnp.full_like(m_s, -jnp.inf)
        acc_s[...] = jnp.zeros_like(acc_s)

    ones = ones_ref[...]
    t = _dot_nt(jnp.concatenate([q_ref[...], ones], axis=1),
                jnp.concatenate([k_ref[...], kb_ref[0]], axis=1))
    v_ext = jnp.concatenate([v_ref[...], ones], axis=1)

    def update(t):
        m_prev = m_s[...]
        m_new = jnp.maximum(m_prev, jnp.max(t, axis=1, keepdims=True))
        pr = jnp.exp(t - m_new)
        acc_s[...] = jnp.exp(m_prev - m_new) * acc_s[...] + _dot(pr.astype(BF16), v_ext)
        m_s[...] = m_new

    @pl.when(ki < qi)
    def _off_diag():
        update(t)

    @pl.when(ki == qi)
    def _diag():
        row = lax.broadcasted_iota(jnp.int32, (tb, tb), 0)
        col = lax.broadcasted_iota(jnp.int32, (tb, tb), 1)
        update(jnp.where(col <= row, t, -jnp.inf))
        acc = acc_s[...]
        o_ref[...] = (acc[:, 0:hd] / acc[:, hd:hd + 1]).astype(o_ref.dtype)


def fox_prompt(q, k, v, neg_cum, tb=FOX_BLOCK):
    T = q.shape[0]
    tb = min(tb, T)
    nb = T // tb
    pairs = [(i, j) for i in range(nb) for j in range(i + 1)]
    qi_tab = jnp.array([a for a, _ in pairs], jnp.int32)
    ki_tab = jnp.array([b for _, b in pairs], jnp.int32)
    n_split = 3
    kb = jnp.stack(_split3(neg_cum.T), axis=-1)
    kb = jnp.pad(kb, ((0, 0), (0, 0), (0, FX_HD - n_split)))
    ones = (lax.broadcasted_iota(jnp.int32, (tb, FX_HD), 1) < n_split).astype(BF16)
    grid_spec = pltpu.PrefetchScalarGridSpec(
        num_scalar_prefetch=2,
        grid=(FX_HEADS, len(pairs)),
        in_specs=[pl.BlockSpec((tb, FX_HD), lambda h, p, qt, kt: (qt[p], h)),
                  pl.BlockSpec((tb, FX_HD), lambda h, p, qt, kt: (kt[p], h)),
                  pl.BlockSpec((tb, FX_HD), lambda h, p, qt, kt: (kt[p], h)),
                  pl.BlockSpec((1, tb, FX_HD), lambda h, p, qt, kt: (h, kt[p], 0)),
                  pl.BlockSpec((tb, FX_HD), lambda h, p, qt, kt: (0, 0))],
        out_specs=pl.BlockSpec((tb, FX_HD), lambda h, p, qt, kt: (qt[p], h)),
        scratch_shapes=[pltpu.VMEM((tb, 1), F32), pltpu.VMEM((tb, 2 * FX_HD), F32)])
    return pl.pallas_call(
        _fox_prompt_body,
        out_shape=jax.ShapeDtypeStruct((T, FX_W), BF16),
        grid_spec=grid_spec,
        compiler_params=_params("parallel", "arbitrary"),
        name="fox_prompt",
    )(qi_tab, ki_tab, q, k, v, kb, ones)


def _fox_decode_body(pt_ref, q_ref, *refs, pages_per_step):
    npg = pages_per_step
    kc_refs, vc_refs, lft_refs = refs[0:npg], refs[npg:2 * npg], refs[2 * npg:3 * npg]
    knew_ref, vnew_ref, lfnew_ref, later_ref, o_ref, m_s, l_s, acc_s, carry_s = refs[3 * npg:]
    pg = pl.program_id(1)
    H = FX_HEADS
    q = q_ref[0]

    @pl.when(pg == 0)
    def _init():
        m_s[...] = jnp.sum(q.astype(F32) * knew_ref[0], axis=1, keepdims=True)
        l_s[...] = jnp.ones_like(l_s)
        acc_s[...] = vnew_ref[0]
        carry_s[...] = jnp.zeros_like(carry_s)

    carry = carry_s[...]
    ts = []
    for kc_ref, lft_ref in zip(kc_refs, lft_refs):
        lpt = lft_ref[0]
        hi, mid, lo = _split3(lpt)
        decay = _dot_nt(jnp.concatenate([hi, mid, lo], axis=0), later_ref[...])
        t = (_dot_nt(q, kc_ref[0].astype(BF16)) + decay[0:H] + decay[H:2 * H] + decay[2 * H:3 * H]
             + (carry + lfnew_ref[0]))
        carry = carry + jnp.sum(lpt, axis=1, keepdims=True)
        sub = lax.broadcasted_iota(jnp.int32, t.shape, 0)
        lane = lax.broadcasted_iota(jnp.int32, t.shape, 1)
        ts.append(jnp.where((lane & (H - 1)) == sub, t, -jnp.inf))
    carry_s[...] = carry
    m_prev = m_s[...]
    m_new = m_prev
    for t in ts:
        m_new = jnp.maximum(m_new, jnp.max(t, axis=1, keepdims=True))
    alpha = jnp.exp(m_prev - m_new)
    l_new = alpha * l_s[...]
    acc = alpha * acc_s[...]
    for t, vc_ref in zip(ts, vc_refs):
        pr = jnp.exp(t - m_new)
        l_new = l_new + jnp.sum(pr, axis=1, keepdims=True)
        acc = acc + _dot(pr.astype(BF16), vc_ref[0].astype(BF16))
    l_s[...] = l_new
    acc_s[...] = acc
    m_s[...] = m_new

    @pl.when(pg == pl.num_programs(1) - 1)
    def _final():
        o_ref[0] = acc_s[...] / l_s[...]


def fox_decode(page_table, q, cache_k, cache_v, logf_t, k_new, v_new, lf_new, later):
    B, n_pages = page_table.shape
    H, hd = FX_HEADS, FX_HD
    assert H & (H - 1) == 0
    rows = PAGE_SIZE * H
    npg = max(n for n in FOX_DECODE_PAGES if n_pages % n == 0)
    page = lambda j: (lambda b, p, pt: (pt[b * n_pages + n_pages - 1 - (p * npg + j)], 0, 0))
    seq = lambda b, p, pt: (b, 0, 0)
    grid_spec = pltpu.PrefetchScalarGridSpec(
        num_scalar_prefetch=1,
        grid=(B, n_pages // npg),
        in_specs=([pl.BlockSpec((1, H, hd), seq)]
                  + [pl.BlockSpec((1, rows, hd), page(j)) for j in range(npg)]
                  + [pl.BlockSpec((1, rows, hd), page(j)) for j in range(npg)]
                  + [pl.BlockSpec((1, H, PAGE_SIZE), page(j)) for j in range(npg)]
                  + [pl.BlockSpec((1, H, hd), seq),
                     pl.BlockSpec((1, H, hd), seq),
                     pl.BlockSpec((1, H, 1), seq),
                     pl.BlockSpec((rows, PAGE_SIZE), lambda b, p, pt: (0, 0))]),
        out_specs=pl.BlockSpec((1, H, hd), seq),
        scratch_shapes=[pltpu.VMEM((H, 1), F32), pltpu.VMEM((H, 1), F32),
                        pltpu.VMEM((H, hd), F32), pltpu.VMEM((H, 1), F32)])
    return pl.pallas_call(
        functools.partial(_fox_decode_body, pages_per_step=npg),
        out_shape=jax.ShapeDtypeStruct((B, H, hd), F32),
        grid_spec=grid_spec,
        compiler_params=_params("parallel", "arbitrary"),
        name="fox_decode",
    )(page_table.reshape(-1), q, *([cache_k] * npg), *([cache_v] * npg), *([logf_t] * npg),
      k_new, v_new, lf_new, later)


def _first_argmax(x, n):
    io = lax.broadcasted_iota(jnp.int32, x.shape, 0)
    m = jnp.max(x, axis=0, keepdims=True)
    i = jnp.min(jnp.where(x == m, io, n), axis=0, keepdims=True)
    return m, i, io


def _stack_rows(rows, dtype):
    n = len(rows)
    io = lax.broadcasted_iota(jnp.int32, (n, rows[0].shape[1]), 0)
    out = jnp.broadcast_to(rows[0], io.shape).astype(dtype)
    for r in range(1, n):
        out = jnp.where(io == r, rows[r].astype(dtype), out)
    return out


def _route_body(h_ref, whi_ref, wlo_ref, b_ref, before_ref, idx_ref, w_ref, rank_ref, cnt_ref, carry_s):
    @pl.when(pl.program_id(0) == 0)
    def _init():
        carry_s[...] = jnp.zeros_like(carry_s)

    h = h_ref[...]
    h_hi = h.astype(BF16)
    h_lo = (h - h_hi.astype(F32)).astype(BF16)
    whi = whi_ref[...]
    logits = _dot_nt(whi, h_hi) + _dot_nt(whi, h_lo) + _dot_nt(wlo_ref[...], h_hi)
    s = jax.nn.sigmoid(logits)
    sb = s + b_ref[...]
    E = sb.shape[0]
    gsz = E // N_GROUPS
    neg = -jnp.inf

    gscore = []
    for g in range(N_GROUPS):
        x = sb[g * gsz:(g + 1) * gsz]
        m1, i1, io = _first_argmax(x, gsz)
        m2 = jnp.max(jnp.where(io == i1, neg, x), axis=0, keepdims=True)
        gscore.append(m1 + m2)
    cur = _stack_rows(gscore, F32)
    keep = jnp.zeros(cur.shape, F32)
    for _ in range(TOPK_GROUPS):
        _, ig, iog = _first_argmax(cur, N_GROUPS)
        hit = iog == ig
        keep = jnp.where(hit, 1.0, keep)
        cur = jnp.where(hit, neg, cur)
    blocks = []
    for g in range(N_GROUPS):
        keep_g = jnp.max(jnp.where(iog == g, keep, 0.0), axis=0, keepdims=True)
        blocks.append(jnp.where(keep_g > 0.0, sb[g * gsz:(g + 1) * gsz], neg))
    cand = jnp.concatenate(blocks, axis=0)

    sel = jnp.zeros(cand.shape, F32)
    idxs, ws = [], []
    for _ in range(TOP_K):
        _, ik, ioe = _first_argmax(cand, E)
        hit = ioe == ik
        idxs.append(ik)
        ws.append(jnp.sum(jnp.where(hit, s, 0.0), axis=0, keepdims=True))
        cand = jnp.where(hit, neg, cand)
        sel = jnp.where(hit, 1.0, sel)
    wsum = ws[0]
    for k in range(1, TOP_K):
        wsum = wsum + ws[k]
    ws = [w / wsum * ROUTED_SCALE for w in ws]

    prefix = _dot(sel.astype(BF16), before_ref[...]) + carry_s[...]
    ranks = [jnp.sum(jnp.where(ioe == ik, prefix, 0.0), axis=0, keepdims=True) for ik in idxs]
    carry_s[...] = carry_s[...] + jnp.sum(sel, axis=1, keepdims=True)
    cnt_ref[...] = carry_s[...]
    idx_ref[...] = _stack_rows(idxs, jnp.int32)
    w_ref[...] = _stack_rows(ws, F32)
    rank_ref[...] = _stack_rows(ranks, F32).astype(jnp.int32)


def route_tokens(h, w_hi_t, w_lo_t, b_col, tm=640):
    T, D = h.shape
    E = w_hi_t.shape[0]
    tm = _row_tile(T, tm)
    tok = lax.broadcasted_iota(jnp.int32, (tm, tm), 0)
    before = (tok < tok.T).astype(BF16)
    kt = lambda i: (0, i)
    const = lambda i: (0, 0)
    return pl.pallas_call(
        _route_body,
        out_shape=[jax.ShapeDtypeStruct((TOP_K, T), jnp.int32), jax.ShapeDtypeStruct((TOP_K, T), F32),
                   jax.ShapeDtypeStruct((TOP_K, T), jnp.int32), jax.ShapeDtypeStruct((E, 1), F32)],
        grid=(T // tm,),
        in_specs=[pl.BlockSpec((tm, D), lambda i: (i, 0)),
                  pl.BlockSpec((E, D), const), pl.BlockSpec((E, D), const),
                  pl.BlockSpec((E, 1), const), pl.BlockSpec((tm, tm), const)],
        out_specs=[pl.BlockSpec((TOP_K, tm), kt), pl.BlockSpec((TOP_K, tm), kt),
                   pl.BlockSpec((TOP_K, tm), kt), pl.BlockSpec((E, 1), const)],
        scratch_shapes=[pltpu.VMEM((E, 1), F32)],
        compiler_params=_params("arbitrary"),
        name="route_tokens",
    )(h, w_hi_t, w_lo_t, b_col, before)


def _slot(ps_ref, idx_ref, rank_ref, k, t):
    return ps_ref[idx_ref[k, t]] + rank_ref[k, t]


def _dispatch_body(ps_ref, idx_ref, rank_ref, h_ref, xs_in_ref, xs_ref, sem):
    del xs_in_ref
    tq = h_ref.shape[0]

    def row_copy(t, k):
        return pltpu.make_async_copy(h_ref.at[pl.ds(t, 1)],
                                     xs_ref.at[pl.ds(_slot(ps_ref, idx_ref, rank_ref, k, t), 1)], sem)

    def start(t, c):
        for k in range(TOP_K):
            row_copy(t, k).start()
        return c

    def wait(t, c):
        for k in range(TOP_K):
            row_copy(t, k).wait()
        return c

    lax.fori_loop(0, tq, start, 0)
    lax.fori_loop(0, tq, wait, 0)


def moe_dispatch(pad_start, idx, rank, h_packed, n_rows, tq=128):
    T, Dp = h_packed.shape
    tq = _row_tile(T, tq)
    smem = lambda: pl.BlockSpec((TOP_K, tq), lambda i, ps: (0, i), memory_space=pltpu.SMEM)
    grid_spec = pltpu.PrefetchScalarGridSpec(
        num_scalar_prefetch=1,
        grid=(T // tq,),
        in_specs=[smem(), smem(),
                  pl.BlockSpec((tq, Dp), lambda i, ps: (i, 0)),
                  pl.BlockSpec(memory_space=pl.ANY)],
        out_specs=pl.BlockSpec(memory_space=pl.ANY),
        scratch_shapes=[pltpu.SemaphoreType.DMA(())])
    return pl.pallas_call(
        _dispatch_body,
        out_shape=jax.ShapeDtypeStruct((n_rows, Dp), h_packed.dtype),
        grid_spec=grid_spec,
        input_output_aliases={4: 0},
        compiler_params=_params("arbitrary"),
        name="moe_dispatch",
    )(pad_start, idx, rank, h_packed, jnp.zeros((n_rows, Dp), h_packed.dtype))


def _combine_body(ps_ref, idx_ref, rank_ref, w_ref, yb_ref, o_ref, buf, sem):
    tq = o_ref.shape[0]

    def row_copy(t, k):
        return pltpu.make_async_copy(yb_ref.at[pl.ds(_slot(ps_ref, idx_ref, rank_ref, k, t), 1)],
                                     buf.at[k, pl.ds(t, 1)], sem)

    def start(t, c):
        for k in range(TOP_K):
            row_copy(t, k).start()
        return c

    def wait(t, c):
        for k in range(TOP_K):
            row_copy(t, k).wait()
        return c

    lax.fori_loop(0, tq, start, 0)
    lax.fori_loop(0, tq, wait, 0)
    acc = w_ref[:, 0:1] * buf[0]
    for k in range(1, TOP_K):
        acc = acc + w_ref[:, k:k + 1] * buf[k]
    o_ref[...] = acc


def moe_combine(pad_start, idx, rank, w_col, yb, tq=128):
    T = w_col.shape[0]
    D = yb.shape[1]
    tq = _row_tile(T, tq)
    smem = lambda: pl.BlockSpec((TOP_K, tq), lambda i, ps: (0, i), memory_space=pltpu.SMEM)
    grid_spec = pltpu.PrefetchScalarGridSpec(
        num_scalar_prefetch=1,
        grid=(T // tq,),
        in_specs=[smem(), smem(),
                  pl.BlockSpec((tq, TOP_K), lambda i, ps: (i, 0)),
                  pl.BlockSpec(memory_space=pl.ANY)],
        out_specs=pl.BlockSpec((tq, D), lambda i, ps: (i, 0)),
        scratch_shapes=[pltpu.VMEM((TOP_K, tq, D), F32), pltpu.SemaphoreType.DMA(())])
    return pl.pallas_call(
        _combine_body,
        out_shape=jax.ShapeDtypeStruct((T, D), F32),
        grid_spec=grid_spec,
        compiler_params=_params("arbitrary"),
        name="moe_combine",
    )(pad_start, idx, rank, w_col, yb)


def _moe_body(seq_ref, tj_ref, tf_ref, na_ref, nv_ref, x_ref, wg_ref, wu_ref, wd_ref, y_ref,
              g_buf, u_buf, d_buf, wg_s, wu_s, wd_s, sem):
    i = pl.program_id(0)

    def fetch(j):
        e = seq_ref[j]
        slot = j % 2
        return [pltpu.make_async_copy(w.at[e], buf.at[slot], sem.at[slot, n])
                for n, (w, buf) in enumerate(((wg_ref, g_buf), (wu_ref, u_buf), (wd_ref, d_buf)))]

    @pl.when(i == 0)
    def _prime():
        for c in fetch(0):
            c.start()

        @pl.when(na_ref[0] > 1)
        def _second():
            for c in fetch(1):
                c.start()

    @pl.when(i >= nv_ref[0])
    def _unused_tile():
        y_ref[...] = jnp.zeros_like(y_ref)

    @pl.when(jnp.logical_and(i < nv_ref[0], tf_ref[i] == 1))
    def _load_expert():
        j = tj_ref[i]
        slot = j % 2
        for c in fetch(j):
            c.wait()
        wg_s[...] = g_buf[slot].astype(BF16)
        wu_s[...] = u_buf[slot].astype(BF16)
        wd_s[...] = d_buf[slot].astype(BF16)

        @pl.when(j + 2 < na_ref[0])
        def _refill():
            for c in fetch(j + 2):
                c.start()

    @pl.when(i < nv_ref[0])
    def _tile():
        xw = x_ref[...]
        half = xw.shape[1]
        x_lo = lax.bitcast_convert_type(xw << 16, F32).astype(BF16)
        x_hi = lax.bitcast_convert_type(xw & jnp.int32(-65536), F32).astype(BF16)
        g = _dot(x_lo, wg_s[0:half]) + _dot(x_hi, wg_s[half:2 * half])
        u = _dot(x_lo, wu_s[0:half]) + _dot(x_hi, wu_s[half:2 * half])
        y_ref[...] = _dot((jax.nn.silu(g) * u).astype(BF16), wd_s[...])


def moe_experts(xs, expert_seq, tile_seq, tile_first, n_active, n_valid, w_gate, w_up, w_down):
    R, Dp = xs.shape
    D = 2 * Dp
    tm = MOE_TILE
    nt = R // tm
    De = w_gate.shape[2]
    grid_spec = pltpu.PrefetchScalarGridSpec(
        num_scalar_prefetch=5,
        grid=(nt,),
        in_specs=[pl.BlockSpec((tm, Dp), lambda i, *_: (i, 0)),
                  pl.BlockSpec(memory_space=pl.ANY),
                  pl.BlockSpec(memory_space=pl.ANY),
                  pl.BlockSpec(memory_space=pl.ANY)],
        out_specs=pl.BlockSpec((tm, D), lambda i, *_: (i, 0)),
        scratch_shapes=[pltpu.VMEM((2, D, De), F32), pltpu.VMEM((2, D, De), F32), pltpu.VMEM((2, De, D), F32),
                        pltpu.VMEM((D, De), BF16), pltpu.VMEM((D, De), BF16), pltpu.VMEM((De, D), BF16),
                        pltpu.SemaphoreType.DMA((2, 3))])
    return pl.pallas_call(
        _moe_body,
        out_shape=jax.ShapeDtypeStruct((R, D), F32),
        grid_spec=grid_spec,
        compiler_params=_params("arbitrary"),
        name="moe_experts",
    )(expert_seq, tile_seq, tile_first, n_active, n_valid, xs, w_gate, w_up, w_down)


def routed_moe(h_f32, h_bf, w_r_hi_t, w_r_lo_t, b_router, w_gate, w_up, w_down):
    T, D = h_f32.shape
    E = N_EXPERTS
    tm = MOE_TILE
    idx, w, rank, cnt = route_tokens(h_f32, w_r_hi_t, w_r_lo_t, b_router.astype(F32).reshape(E, 1))

    counts = cnt[:, 0].astype(jnp.int32)
    padded = (counts + tm - 1) // tm * tm
    pad_end = jnp.cumsum(padded)
    pad_start = pad_end - padded
    nt = -(-(T * TOP_K) // tm) + E
    n_valid = (pad_end[-1] // tm).astype(jnp.int32)
    tile_row = jnp.minimum(jnp.arange(nt, dtype=jnp.int32), n_valid - 1)
    tile_expert = jnp.minimum(jnp.searchsorted(pad_end, tile_row * tm, side='right'), E - 1).astype(jnp.int32)
    tile_first = jnp.concatenate([jnp.ones((1,), jnp.int32),
                                  (tile_expert[1:] != tile_expert[:-1]).astype(jnp.int32)])
    active = counts > 0
    expert_seq = jnp.nonzero(active, size=E, fill_value=0)[0].astype(jnp.int32)
    tile_seq = (jnp.cumsum(active.astype(jnp.int32)) - 1)[tile_expert]
    n_active = jnp.sum(active.astype(jnp.int32))

    half = D // 2
    bits = lax.bitcast_convert_type(h_bf, jnp.uint16).astype(jnp.uint32)
    h_packed = lax.bitcast_convert_type(bits[:, :half] | (bits[:, half:] << 16), jnp.int32)
    xs = moe_dispatch(pad_start, idx, rank, h_packed, nt * tm)
    yb = moe_experts(xs, expert_seq, tile_seq, tile_first, n_active.reshape(1), n_valid.reshape(1),
                     w_gate, w_up, w_down)
    return moe_combine(pad_start, idx, rank, w.T, yb)


def _prep_weights(w_in, b_ml_i, b_ml_f, b_fx_f, w_br_ml, w_br_fx, w_out, w_router,
                  w_sh_gate, w_sh_up, w_sh_down, w_ple_gate, w_ple_proj):
    o_ml = 2 * ML_W_QK + 2 * ML_W_V
    o_fx = o_ml + 2 * ML_HEADS
    o_ff = o_fx + 3 * FX_W
    o_g = o_ff + FX_HEADS
    wb = w_in.astype(BF16)
    small = jnp.concatenate([wb[:, o_ml:o_fx], wb[:, o_ff:o_g]], axis=1)
    small = jnp.pad(small, ((0, 0), (0, LANES - small.shape[1])))
    w_r_hi = w_router.T.astype(BF16)
    w_r_lo = (w_router.T - w_r_hi.astype(F32)).astype(BF16)
    return dict(
        w_ml=wb[:, :o_ml], w_fq=wb[:, o_fx:o_fx + FX_W], w_fk=wb[:, o_fx + FX_W:o_fx + 2 * FX_W],
        w_fv=wb[:, o_fx + 2 * FX_W:o_ff], w_g=wb[:, o_g:], w_small=small,
        w_br_ml=w_br_ml.astype(BF16), w_br_fx=w_br_fx.astype(BF16), w_out=w_out.astype(BF16),
        w_r_hi=w_r_hi, w_r_lo=w_r_lo,
        w_sh_gate=w_sh_gate.astype(BF16), w_sh_up=w_sh_up.astype(BF16), w_sh_down=w_sh_down.astype(BF16),
        w_ple_gate=w_ple_gate.astype(BF16), w_ple_proj=w_ple_proj.astype(BF16))


def _in_proj(x_bf, W):
    (z_ml,) = matmul(x_bf, W['w_ml'], name="proj_ml")
    (fq,) = matmul(x_bf, W['w_fq'], out_dtypes=(BF16,), scale=FX_SCALE, name="proj_fq")
    fk, fk_bf = matmul(x_bf, W['w_fk'], out_dtypes=(F32, BF16), name="proj_fk")
    fv, fv_bf = matmul(x_bf, W['w_fv'], out_dtypes=(F32, BF16), name="proj_fv")
    (z_g,) = matmul(x_bf, W['w_g'], name="proj_gates")
    (z_s,) = matmul(x_bf, W['w_small'], name="proj_small")
    return z_ml, fq, fk, fk_bf, fv, fv_bf, z_g, z_s


def _small_gates(z_s, b_ml_i, b_ml_f, b_fx_f):
    H = ML_HEADS
    li = z_s[:, 0:H] + b_ml_i
    lf = jax.nn.log_sigmoid(z_s[:, H:2 * H] + b_ml_f)
    lf_fx = jax.nn.log_sigmoid(z_s[:, 2 * H:2 * H + FX_HEADS] + b_fx_f)
    return li, lf, lf_fx


def kernel(x_prompt, x_sample, p_prompt, p_sample, cache_k, cache_v, cache_logf, state_C, state_n, state_m,
           page_table, w_in, b_ml_i, b_ml_f, b_fx_f, ml_norm_g, w_br_ml, w_br_fx, w_out, ln1_g, ln1_b,
           w_router, b_router, w_exp_gate, w_exp_up, w_exp_down, w_sh_gate, w_sh_up, w_sh_down,
           w_ple_gate, w_ple_proj, ln2_g, ln2_b):
    T = x_prompt.shape[1]
    B = x_sample.shape[0]
    H = ML_HEADS
    W = _prep_weights(w_in[0], b_ml_i[0], b_ml_f[0], b_fx_f[0], w_br_ml[0], w_br_fx[0], w_out[0], w_router[0],
                      w_sh_gate[0], w_sh_up[0], w_sh_down[0], w_ple_gate[0], w_ple_proj[0])
    xp = x_prompt[0]
    xs = x_sample[:, 0]

    z_ml, fq, fk, fk_bf, fv, fv_bf, z_g, z_s = _in_proj(xp.astype(BF16), W)
    li, lf, lf_fx = _small_gates(z_s, b_ml_i[0], b_ml_f[0], b_fx_f[0])
    nc = T // ML_CHUNK
    to_rows = lambda a: a.T.reshape(H, nc, 1, ML_CHUNK)
    hm_p, C_p, n_p, m_p = mlstm_prompt(z_ml, to_rows(li), to_rows(lf), ml_norm_g)
    hfx_p = fox_prompt(fq, fk_bf, fv_bf, -jnp.cumsum(lf_fx, axis=0))
    merged_p = gated_merge(hm_p, W['w_br_ml'], hfx_p, W['w_br_fx'], z_g, BF16)
    h1_p, h1b_p = out_proj_ln(merged_p, W['w_out'], xp, ln1_g, ln1_b)

    zd_ml, dq, dk, dk_bf, dv, dv_bf, zd_g, zd_s = _in_proj(xs.astype(BF16), W)
    dli, dlf, dlf_fx = _small_gates(zd_s, b_ml_i[0], b_ml_f[0], b_fx_f[0])
    col = lambda a: a.reshape(B, H, ML_DQK, 1)
    gates = jnp.stack([dli, dlf, state_m[0]], axis=-1)
    gates = jnp.pad(gates, ((0, 0), (0, 0), (0, LANES - 3))).reshape(B, H, 1, LANES)
    hm_d, C_d, n_d, m_d = mlstm_decode(
        state_C[0], col(state_n[0]), col(zd_ml[:, :ML_W_QK]), col(zd_ml[:, ML_W_QK:2 * ML_W_QK]),
        zd_ml[:, 2 * ML_W_QK:2 * ML_W_QK + ML_W_V].reshape(B, H, 1, ML_DV),
        zd_ml[:, 2 * ML_W_QK + ML_W_V:].reshape(B, H, 1, ML_DV),
        gates, ml_norm_g.reshape(H, 1, ML_DV))
    n_pool = cache_k.shape[1]
    page_rows = PAGE_SIZE * FX_HEADS
    row_of = jnp.arange(page_rows, dtype=jnp.int32) // FX_HEADS
    later = (jnp.arange(PAGE_SIZE, dtype=jnp.int32)[None, :] > row_of[:, None]).astype(BF16)
    heads = lambda a: a.reshape(B, FX_HEADS, FX_HD)
    hfx_d = fox_decode(page_table, heads(dq), cache_k.reshape(n_pool, page_rows, FX_HD),
                       cache_v.reshape(n_pool, page_rows, FX_HD), jnp.swapaxes(cache_logf[0], 1, 2),
                       heads(dk), heads(dv), dlf_fx[:, :, None], later)
    merged_d = gated_merge(hm_d.reshape(B, ML_W_V), W['w_br_ml'], hfx_d.reshape(B, FX_W).astype(BF16),
                           W['w_br_fx'], zd_g, BF16)
    h1_d, h1b_d = out_proj_ln(merged_d, W['w_out'], xs, ln1_g, ln1_b)

    h1_all = jnp.concatenate([h1_p, h1_d], axis=0)
    h1b_all = jnp.concatenate([h1b_p, h1b_d], axis=0)
    moe = routed_moe(h1_all, h1b_all, W['w_r_hi'], W['w_r_lo'], b_router[0],
                     w_exp_gate[0], w_exp_up[0], w_exp_down[0])
    ple_p = ple_embed(h1b_p, W['w_ple_gate'], p_prompt[0, 0].astype(BF16), W['w_ple_proj'])
    ple_d = ple_embed(h1b_d, W['w_ple_gate'], p_sample[0, :, 0].astype(BF16), W['w_ple_proj'])
    y_p = ffn_tail(h1b_p, h1_p, moe[:T], ple_p, W['w_sh_gate'], W['w_sh_up'], W['w_sh_down'], ln2_g, ln2_b)
    y_d = ffn_tail(h1b_d, h1_d, moe[T:], ple_d, W['w_sh_gate'], W['w_sh_up'], W['w_sh_down'], ln2_g, ln2_b)

    return (y_p[None], y_d[:, None],
            fk.reshape(1, 1, T, FX_HEADS, FX_HD), fv.reshape(1, 1, T, FX_HEADS, FX_HD), lf_fx[None, None],
            C_p[None, None], n_p.reshape(1, 1, H, ML_DQK), m_p[:, 0, 0].reshape(1, 1, H),
            dk.reshape(1, B, 1, FX_HEADS, FX_HD), dv.reshape(1, B, 1, FX_HEADS, FX_HD), dlf_fx[None, :, None],
            C_d[None], n_d.reshape(1, B, H, ML_DQK), m_d[:, :, 0, 0][None])
```

```python
import functools

import jax
import jax.numpy as jnp
from jax import lax
from jax.experimental import pallas as pl
from jax.experimental.pallas import tpu as pltpu

F32 = jnp.float32
BF16 = jnp.bfloat16

D_MODEL = 2048
ML_HEADS = 4
ML_DQK = 256
ML_DV = 512
ML_CHUNK = 128
ML_W_QK = ML_HEADS * ML_DQK
ML_W_V = ML_HEADS * ML_DV
FX_HEADS = 16
FX_HD = 128
FX_W = FX_HEADS * FX_HD
FX_SCALE = FX_HD ** -0.5
PAGE_SIZE = 128
N_EXPERTS = 256
TOP_K = 8
N_GROUPS = 8
TOPK_GROUPS = 4
D_EXPERT = 512
ROUTED_SCALE = 2.5
LN_EPS = 1e-5
DEEPNORM_ALPHA = 2.0 ** 0.25

LANES = 128
SUBLANES = 8
VMEM_LIMIT = 56 * 1024 * 1024
MOE_TILE = 320
MOE_DMA_TOKENS = 128
FOX_BLOCK = 1024
FOX_DECODE_PAGES = (4, 2, 1)


def _params(*sem):
    return pltpu.CompilerParams(dimension_semantics=sem, vmem_limit_bytes=VMEM_LIMIT)


def _row_tile(m, target):
    if m <= target:
        return m
    best = max(t for t in range(LANES, target + 1, LANES) if m % t == 0)
    assert m % best == 0
    return best


def _dot(a, b):
    return jnp.dot(a, b, preferred_element_type=F32)


def _dot_nt(a, b):
    return lax.dot_general(a, b, (((1,), (1,)), ((), ())), preferred_element_type=F32)


def _split3(x):
    hi = x.astype(BF16)
    r1 = x - hi.astype(F32)
    mid = r1.astype(BF16)
    lo = (r1 - mid.astype(F32)).astype(BF16)
    return hi, mid, lo


def _round_bf16(x):
    return lax.reduce_precision(x, exponent_bits=8, mantissa_bits=7)


def _split3_traced(x):
    hi = _round_bf16(x)
    mid = _round_bf16(x - hi)
    lo = _round_bf16(x - hi - mid)
    return hi.astype(BF16), mid.astype(BF16), lo.astype(BF16)


def _layer_norm(y, g, b):
    mu = jnp.mean(y, axis=-1, keepdims=True)
    d = y - mu
    var = jnp.mean(d * d, axis=-1, keepdims=True)
    return d * lax.rsqrt(var + LN_EPS) * g + b


def _mm_body(a_ref, b_ref, *o_refs, scale):
    acc = _dot(a_ref[...], b_ref[...])
    if scale != 1.0:
        acc = acc * scale
    for o in o_refs:
        o[...] = acc.astype(o.dtype)


def matmul(a, b, out_dtypes=(F32,), tm=512, tn=1024, scale=1.0, name="matmul"):
    M, K = a.shape
    N = b.shape[1]
    tm, tn = _row_tile(M, tm), _row_tile(N, tn)
    outs = pl.pallas_call(
        functools.partial(_mm_body, scale=scale),
        out_shape=[jax.ShapeDtypeStruct((M, N), d) for d in out_dtypes],
        grid=(M // tm, N // tn),
        in_specs=[pl.BlockSpec((tm, K), lambda i, j: (i, 0)),
                  pl.BlockSpec((K, tn), lambda i, j: (0, j))],
        out_specs=[pl.BlockSpec((tm, tn), lambda i, j: (i, j)) for _ in out_dtypes],
        compiler_params=_params("parallel", "parallel"),
        name=name,
    )(a, b)
    return outs


def _pair_body(a1_ref, b1_ref, a2_ref, b2_ref, g1_ref, g2_ref, o_ref, *, mode):
    y1 = _dot(a1_ref[...], b1_ref[...])
    y2 = _dot(a2_ref[...], b2_ref[...])
    if mode == "merge":
        out = jax.nn.sigmoid(g1_ref[...]) * y1 + jax.nn.sigmoid(g2_ref[...]) * y2
    else:
        out = jax.nn.sigmoid(y1) * y2
    o_ref[...] = out.astype(o_ref.dtype)


def gated_merge(a1, b1, a2, b2, gates, out_dtype, tm=512, tn=512):
    M, K = a1.shape
    N = b1.shape[1]
    tm, tn = _row_tile(M, tm), _row_tile(N, tn)
    nj = N // tn
    return pl.pallas_call(
        functools.partial(_pair_body, mode="merge"),
        out_shape=jax.ShapeDtypeStruct((M, N), out_dtype),
        grid=(M // tm, nj),
        in_specs=[pl.BlockSpec((tm, K), lambda i, j: (i, 0)),
                  pl.BlockSpec((K, tn), lambda i, j: (0, j)),
                  pl.BlockSpec((tm, K), lambda i, j: (i, 0)),
                  pl.BlockSpec((K, tn), lambda i, j: (0, j)),
                  pl.BlockSpec((tm, tn), lambda i, j: (i, j)),
                  pl.BlockSpec((tm, tn), lambda i, j: (i, j + nj))],
        out_specs=pl.BlockSpec((tm, tn), lambda i, j: (i, j)),
        compiler_params=_params("parallel", "parallel"),
        name="gated_merge",
    )(a1, b1, a2, b2, gates, gates)


def _ple_body(a1_ref, b1_ref, a2_ref, b2_ref, o_ref):
    y1 = _dot(a1_ref[...], b1_ref[...])
    y2 = _dot(a2_ref[...], b2_ref[...])
    o_ref[...] = (jax.nn.sigmoid(y1) * y2).astype(o_ref.dtype)


def ple_embed(h, w_gate, p, w_proj, tm=512, tn=512):
    M, K1 = h.shape
    K2 = p.shape[1]
    N = w_gate.shape[1]
    tm, tn = _row_tile(M, tm), _row_tile(N, tn)
    return pl.pallas_call(
        _ple_body,
        out_shape=jax.ShapeDtypeStruct((M, N), F32),
        grid=(M // tm, N // tn),
        in_specs=[pl.BlockSpec((tm, K1), lambda i, j: (i, 0)),
                  pl.BlockSpec((K1, tn), lambda i, j: (0, j)),
                  pl.BlockSpec((tm, K2), lambda i, j: (i, 0)),
                  pl.BlockSpec((K2, tn), lambda i, j: (0, j))],
        out_specs=pl.BlockSpec((tm, tn), lambda i, j: (i, j)),
        compiler_params=_params("parallel", "parallel"),
        name="ple_embed",
    )(h, w_gate, p, w_proj)


def _outln_body(mg_ref, w_ref, x_ref, g_ref, b_ref, hf_ref, hb_ref, hp_ref):
    y = DEEPNORM_ALPHA * x_ref[...] + _dot(mg_ref[...], w_ref[...])
    h = _layer_norm(y, g_ref[...], b_ref[...])
    hf_ref[...] = h
    hb = h.astype(BF16)
    hb_ref[...] = hb
    half = h.shape[1] // 2
    bits = lax.bitcast_convert_type(hb.astype(F32), jnp.int32)
    hp_ref[...] = lax.shift_right_logical(bits[:, :half], 16) | (bits[:, half:] & jnp.int32(-65536))


def out_proj_ln(merged, w_out, x, g, b, tm=256):
    M, K = merged.shape
    N = w_out.shape[1]
    tm = _row_tile(M, tm)
    return pl.pallas_call(
        _outln_body,
        out_shape=[jax.ShapeDtypeStruct((M, N), F32), jax.ShapeDtypeStruct((M, N), BF16),
                   jax.ShapeDtypeStruct((M, N // 2), jnp.int32)],
        grid=(M // tm,),
        in_specs=[pl.BlockSpec((tm, K), lambda i: (i, 0)),
                  pl.BlockSpec((K, N), lambda i: (0, 0)),
                  pl.BlockSpec((tm, N), lambda i: (i, 0)),
                  pl.BlockSpec((1, N), lambda i: (0, 0)),
                  pl.BlockSpec((1, N), lambda i: (0, 0))],
        out_specs=[pl.BlockSpec((tm, N), lambda i: (i, 0)),
                   pl.BlockSpec((tm, N), lambda i: (i, 0)),
                   pl.BlockSpec((tm, N // 2), lambda i: (i, 0))],
        compiler_params=_params("parallel"),
        name="out_proj_ln",
    )(merged, w_out, x, g, b)


def _ffn_tail_body(hb_ref, hf_ref, moe_ref, ple_ref, wg_ref, wu_ref, wd_ref, g_ref, b_ref, y_ref):
    hb = hb_ref[...]
    sg = _dot(hb, wg_ref[...])
    su = _dot(hb, wu_ref[...])
    sh = _dot((jax.nn.silu(sg) * su).astype(BF16), wd_ref[...])
    y = DEEPNORM_ALPHA * hf_ref[...] + moe_ref[...] + sh + ple_ref[...]
    y_ref[...] = _layer_norm(y, g_ref[...], b_ref[...])


def ffn_tail(h_bf, h_f32, moe, ple, wg, wu, wd, g, b, tm=256):
    M, D = h_bf.shape
    Ds = wg.shape[1]
    tm = _row_tile(M, tm)
    row = lambda i: (i, 0)
    const = lambda i: (0, 0)
    return pl.pallas_call(
        _ffn_tail_body,
        out_shape=jax.ShapeDtypeStruct((M, D), F32),
        grid=(M // tm,),
        in_specs=[pl.BlockSpec((tm, D), row), pl.BlockSpec((tm, D), row),
                  pl.BlockSpec((tm, D), row), pl.BlockSpec((tm, D), row),
                  pl.BlockSpec((D, Ds), const), pl.BlockSpec((D, Ds), const),
                  pl.BlockSpec((Ds, D), const),
                  pl.BlockSpec((1, D), const), pl.BlockSpec((1, D), const)],
        out_specs=pl.BlockSpec((tm, D), row),
        compiler_params=_params("parallel"),
        name="ffn_tail",
    )(h_bf, h_f32, moe, ple, wg, wu, wd, g, b)


def _mlstm_prompt_body(q_ref, k_ref, v_ref, o_ref, li_ref, lf_ref, g_ref,
                       hm_ref, c_out, n_out, m_out, c_s, n_s, m_s):
    c = pl.program_id(1)
    L = ML_CHUNK

    @pl.when(c == 0)
    def _init():
        c_s[...] = jnp.zeros_like(c_s)
        n_s[...] = jnp.zeros_like(n_s)
        m_s[...] = jnp.zeros_like(m_s)

    q = q_ref[...]
    k = k_ref[...] * (ML_DQK ** -0.5)
    v_bf = v_ref[...].astype(BF16)
    q_bf = q.astype(BF16)
    li_r = li_ref[0, 0]
    lf_r = lf_ref[0, 0]

    row = lax.broadcasted_iota(jnp.int32, (L, L), 0)
    col = lax.broadcasted_iota(jnp.int32, (L, L), 1)
    eye = row == col
    causal = col <= row
    lf_c = jnp.sum(jnp.where(eye, lf_r, 0.0), axis=1, keepdims=True)
    b_c = jnp.sum(jnp.where(causal, lf_r, 0.0), axis=1, keepdims=True)
    b_r = jnp.sum(jnp.where(row <= col, lf_c, 0.0), axis=0, keepdims=True)
    b_tot = jnp.sum(lf_r, axis=1, keepdims=True)

    m_prev = m_s[:, 0:1]
    dlog = jnp.where(causal, b_c - b_r + li_r, -jnp.inf)
    inter = b_c + m_prev
    m_row = jnp.maximum(inter, jnp.max(dlog, axis=1, keepdims=True))
    w_inter = jnp.exp(inter - m_row)
    s = _dot_nt(q_bf, k.astype(BF16)) * jnp.exp(dlog - m_row)
    num = w_inter * _dot(q_bf, c_s[...].astype(BF16)) + _dot(s.astype(BF16), v_bf)
    den = w_inter * jnp.sum(q * n_s[...], axis=1, keepdims=True) + jnp.sum(s, axis=1, keepdims=True)
    h = num / jnp.maximum(jnp.abs(den), jnp.exp(-m_row))

    hn = h * lax.rsqrt(jnp.mean(h * h, axis=1, keepdims=True) + LN_EPS)
    hm_ref[...] = (hn * g_ref[...] * jax.nn.sigmoid(o_ref[...])).astype(hm_ref.dtype)

    lw_r = b_tot - b_r + li_r
    m_new = jnp.maximum(b_tot + m_prev, jnp.max(lw_r, axis=1, keepdims=True))
    decay = jnp.exp(b_tot + m_prev - m_new)
    wgt_r = jnp.exp(lw_r - m_new)
    wgt_c = jnp.sum(jnp.where(eye, wgt_r, 0.0), axis=1, keepdims=True)
    wk = k * wgt_c
    c_new = decay * c_s[...] + _dot(wk.T.astype(BF16), v_bf)
    n_new = decay * n_s[...] + jnp.sum(wk, axis=0, keepdims=True)
    c_s[...] = c_new
    n_s[...] = n_new
    m_s[...] = jnp.broadcast_to(m_new, m_s.shape)

    @pl.when(c == pl.num_programs(1) - 1)
    def _final():
        c_out[0] = c_new
        n_out[0] = n_new
        m_out[0] = jnp.broadcast_to(m_new, (1, LANES))


def mlstm_prompt(z, li_rows, lf_rows, norm_g):
    T = z.shape[0]
    nc = T // ML_CHUNK
    kq = ML_W_QK // ML_DQK
    kv = 2 * ML_W_QK // ML_DV
    ko = kv + ML_HEADS
    return pl.pallas_call(
        _mlstm_prompt_body,
        out_shape=[jax.ShapeDtypeStruct((T, ML_W_V), BF16),
                   jax.ShapeDtypeStruct((ML_HEADS, ML_DQK, ML_DV), F32),
                   jax.ShapeDtypeStruct((ML_HEADS, 1, ML_DQK), F32),
                   jax.ShapeDtypeStruct((ML_HEADS, 1, LANES), F32)],
        grid=(ML_HEADS, nc),
        in_specs=[pl.BlockSpec((ML_CHUNK, ML_DQK), lambda h, c: (c, h)),
                  pl.BlockSpec((ML_CHUNK, ML_DQK), lambda h, c: (c, kq + h)),
                  pl.BlockSpec((ML_CHUNK, ML_DV), lambda h, c: (c, kv + h)),
                  pl.BlockSpec((ML_CHUNK, ML_DV), lambda h, c: (c, ko + h)),
                  pl.BlockSpec((1, 1, 1, ML_CHUNK), lambda h, c: (h, c, 0, 0)),
                  pl.BlockSpec((1, 1, 1, ML_CHUNK), lambda h, c: (h, c, 0, 0)),
                  pl.BlockSpec((1, ML_DV), lambda h, c: (0, h))],
        out_specs=[pl.BlockSpec((ML_CHUNK, ML_DV), lambda h, c: (c, h)),
                   pl.BlockSpec((1, ML_DQK, ML_DV), lambda h, c: (h, 0, 0)),
                   pl.BlockSpec((1, 1, ML_DQK), lambda h, c: (h, 0, 0)),
                   pl.BlockSpec((1, 1, LANES), lambda h, c: (h, 0, 0))],
        scratch_shapes=[pltpu.VMEM((ML_DQK, ML_DV), F32),
                        pltpu.VMEM((1, ML_DQK), F32),
                        pltpu.VMEM((1, LANES), F32)],
        compiler_params=_params("parallel", "arbitrary"),
        name="mlstm_prompt",
    )(z, z, z, z, li_rows, lf_rows, norm_g)


def _mlstm_decode_body(c_ref, n_ref, q_ref, k_ref, v_ref, o_ref, gate_ref, g_ref,
                       hm_ref, c_out, n_out, m_out):
    for h in range(ML_HEADS):
        C = c_ref[0, h]
        n_c = n_ref[0, h]
        q_c = q_ref[0, h]
        k_c = k_ref[0, h] * (ML_DQK ** -0.5)
        v_r = v_ref[0, h]
        gt = gate_ref[0, h]
        li, lf, m0 = gt[:, 0:1], gt[:, 1:2], gt[:, 2:3]
        inter = lf + m0
        m_new = jnp.maximum(inter, li)
        w_inter = jnp.exp(inter - m_new)
        w_in = jnp.exp(li - m_new)
        qk = jnp.sum(q_c * k_c, axis=0, keepdims=True) * w_in
        qc = jnp.sum(C * q_c, axis=0, keepdims=True)
        num = w_inter * qc + qk * v_r
        den = w_inter * jnp.sum(q_c * n_c, axis=0, keepdims=True) + qk
        hh = num / jnp.maximum(jnp.abs(den), jnp.exp(-m_new))
        hn = hh * lax.rsqrt(jnp.mean(hh * hh, axis=1, keepdims=True) + LN_EPS)
        hm_ref[0, h] = (hn * g_ref[h] * jax.nn.sigmoid(o_ref[0, h])).astype(hm_ref.dtype)
        wk = k_c * w_in
        c_out[0, h] = w_inter * C + wk * v_r
        n_out[0, h] = w_inter * n_c + wk
        m_out[0, h] = jnp.broadcast_to(m_new, (1, LANES))


def mlstm_decode(state_c, state_n_col, q_col, k_col, v_row, o_row, gates, norm_g):
    B = state_c.shape[0]
    H = ML_HEADS
    blk4 = lambda *s: pl.BlockSpec((1, H) + s, lambda b: (b, 0, 0, 0))
    return pl.pallas_call(
        _mlstm_decode_body,
        out_shape=[jax.ShapeDtypeStruct((B, H, 1, ML_DV), BF16),
                   jax.ShapeDtypeStruct((B, H, ML_DQK, ML_DV), F32),
                   jax.ShapeDtypeStruct((B, H, ML_DQK, 1), F32),
                   jax.ShapeDtypeStruct((B, H, 1, LANES), F32)],
        grid=(B,),
        in_specs=[blk4(ML_DQK, ML_DV), blk4(ML_DQK, 1), blk4(ML_DQK, 1), blk4(ML_DQK, 1),
                  blk4(1, ML_DV), blk4(1, ML_DV), blk4(1, LANES),
                  pl.BlockSpec((H, 1, ML_DV), lambda b: (0, 0, 0))],
        out_specs=[blk4(1, ML_DV), blk4(ML_DQK, ML_DV), blk4(ML_DQK, 1), blk4(1, LANES)],
        compiler_params=_params("parallel"),
        name="mlstm_decode",
    )(state_c, state_n_col, q_col, k_col, v_row, o_row, gates, norm_g)


def _fox_prompt_body(qi_tab, ki_tab, q_ref, k_ref, v_ref, kb_ref, ones_ref, o_ref, m_s, acc_s):
    p = pl.program_id(1)
    qi = qi_tab[p]
    ki = ki_tab[p]
    tb = q_ref.shape[0]
    hd = q_ref.shape[1]

    @pl.when(ki == 0)
    def _init():
        m_s[...] = jnp.full_like(m_s, -jnp.inf)
        acc_s[...] = jnp.zeros_like(acc_s)

    ones = ones_ref[...]
    t = _dot_nt(jnp.concatenate([q_ref[...], ones], axis=1),
                jnp.concatenate([k_ref[...], kb_ref[0]], axis=1))
    v_ext = jnp.concatenate([v_ref[...], ones], axis=1)

    def update(t):
        m_prev = m_s[...]
        m_new = jnp.maximum(m_prev, jnp.max(t, axis=1, keepdims=True))
        pr = jnp.exp(t - m_new)
        acc_s[...] = jnp.exp(m_prev - m_new) * acc_s[...] + _dot(pr.astype(BF16), v_ext)
        m_s[...] = m_new

    @pl.when(ki < qi)
    def _off_diag():
        update(t)

    @pl.when(ki == qi)
    def _diag():
        row = lax.broadcasted_iota(jnp.int32, (tb, tb), 0)
        col = lax.broadcasted_iota(jnp.int32, (tb, tb), 1)
        update(jnp.where(col <= row, t, -jnp.inf))
        acc = acc_s[...]
        o_ref[...] = (acc[:, 0:hd] / acc[:, hd:hd + 1]).astype(o_ref.dtype)


def fox_prompt(q, k, v, neg_cum, tb=FOX_BLOCK):
    T = q.shape[0]
    tb = min(tb, T)
    nb = T // tb
    pairs = [(i, j) for i in range(nb) for j in range(i + 1)]
    qi_tab = jnp.array([a for a, _ in pairs], jnp.int32)
    ki_tab = jnp.array([b for _, b in pairs], jnp.int32)
    n_split = 3
    kb = jnp.stack(_split3_traced(neg_cum.T), axis=-1)
    kb = jnp.pad(kb, ((0, 0), (0, 0), (0, FX_HD - n_split)))
    ones = (lax.broadcasted_iota(jnp.int32, (tb, FX_HD), 1) < n_split).astype(BF16)
    grid_spec = pltpu.PrefetchScalarGridSpec(
        num_scalar_prefetch=2,
        grid=(FX_HEADS, len(pairs)),
        in_specs=[pl.BlockSpec((tb, FX_HD), lambda h, p, qt, kt: (qt[p], h)),
                  pl.BlockSpec((tb, FX_HD), lambda h, p, qt, kt: (kt[p], h)),
                  pl.BlockSpec((tb, FX_HD), lambda h, p, qt, kt: (kt[p], h)),
                  pl.BlockSpec((1, tb, FX_HD), lambda h, p, qt, kt: (h, kt[p], 0)),
                  pl.BlockSpec((tb, FX_HD), lambda h, p, qt, kt: (0, 0))],
        out_specs=pl.BlockSpec((tb, FX_HD), lambda h, p, qt, kt: (qt[p], h)),
        scratch_shapes=[pltpu.VMEM((tb, 1), F32), pltpu.VMEM((tb, 2 * FX_HD), F32)])
    return pl.pallas_call(
        _fox_prompt_body,
        out_shape=jax.ShapeDtypeStruct((T, FX_W), BF16),
        grid_spec=grid_spec,
        compiler_params=_params("parallel", "arbitrary"),
        name="fox_prompt",
    )(qi_tab, ki_tab, q, k, v, kb, ones)


def _fox_decode_body(pt_ref, q_ref, *refs, pages_per_step):
    npg = pages_per_step
    kc_refs, vc_refs, lft_refs = refs[0:npg], refs[npg:2 * npg], refs[2 * npg:3 * npg]
    knew_ref, vnew_ref, lfnew_ref, later_ref, o_ref, m_s, l_s, acc_s, carry_s = refs[3 * npg:]
    pg = pl.program_id(1)
    H = FX_HEADS
    q = q_ref[0]

    @pl.when(pg == 0)
    def _init():
        m_s[...] = jnp.sum(q.astype(F32) * knew_ref[0], axis=1, keepdims=True)
        l_s[...] = jnp.ones_like(l_s)
        acc_s[...] = vnew_ref[0]
        carry_s[...] = jnp.zeros_like(carry_s)

    carry = carry_s[...]
    ts = []
    for kc_ref, lft_ref in zip(kc_refs, lft_refs):
        lpt = lft_ref[0]
        hi, mid, lo = _split3(lpt)
        decay = _dot_nt(jnp.concatenate([hi, mid, lo], axis=0), later_ref[...])
        t = (_dot_nt(q, kc_ref[0].astype(BF16)) + decay[0:H] + decay[H:2 * H] + decay[2 * H:3 * H]
             + (carry + lfnew_ref[0]))
        carry = carry + jnp.sum(lpt, axis=1, keepdims=True)
        sub = lax.broadcasted_iota(jnp.int32, t.shape, 0)
        lane = lax.broadcasted_iota(jnp.int32, t.shape, 1)
        ts.append(jnp.where((lane & (H - 1)) == sub, t, -jnp.inf))
    carry_s[...] = carry
    m_prev = m_s[...]
    m_new = m_prev
    for t in ts:
        m_new = jnp.maximum(m_new, jnp.max(t, axis=1, keepdims=True))
    alpha = jnp.exp(m_prev - m_new)
    l_new = alpha * l_s[...]
    acc = alpha * acc_s[...]
    for t, vc_ref in zip(ts, vc_refs):
        pr = jnp.exp(t - m_new)
        l_new = l_new + jnp.sum(pr, axis=1, keepdims=True)
        acc = acc + _dot(pr.astype(BF16), vc_ref[0].astype(BF16))
    l_s[...] = l_new
    acc_s[...] = acc
    m_s[...] = m_new

    @pl.when(pg == pl.num_programs(1) - 1)
    def _final():
        o_ref[0] = acc_s[...] / l_s[...]


def fox_decode(page_table, q, cache_k, cache_v, logf_t, k_new, v_new, lf_new, later):
    B, n_pages = page_table.shape
    H, hd = FX_HEADS, FX_HD
    assert H & (H - 1) == 0
    rows = PAGE_SIZE * H
    npg = max(n for n in FOX_DECODE_PAGES if n_pages % n == 0)
    page = lambda j: (lambda b, p, pt: (pt[b * n_pages + n_pages - 1 - (p * npg + j)], 0, 0))
    seq = lambda b, p, pt: (b, 0, 0)
    grid_spec = pltpu.PrefetchScalarGridSpec(
        num_scalar_prefetch=1,
        grid=(B, n_pages // npg),
        in_specs=([pl.BlockSpec((1, H, hd), seq)]
                  + [pl.BlockSpec((1, rows, hd), page(j)) for j in range(npg)]
                  + [pl.BlockSpec((1, rows, hd), page(j)) for j in range(npg)]
                  + [pl.BlockSpec((1, H, PAGE_SIZE), page(j)) for j in range(npg)]
                  + [pl.BlockSpec((1, H, hd), seq),
                     pl.BlockSpec((1, H, hd), seq),
                     pl.BlockSpec((1, H, 1), seq),
                     pl.BlockSpec((rows, PAGE_SIZE), lambda b, p, pt: (0, 0))]),
        out_specs=pl.BlockSpec((1, H, hd), seq),
        scratch_shapes=[pltpu.VMEM((H, 1), F32), pltpu.VMEM((H, 1), F32),
                        pltpu.VMEM((H, hd), F32), pltpu.VMEM((H, 1), F32)])
    return pl.pallas_call(
        functools.partial(_fox_decode_body, pages_per_step=npg),
        out_shape=jax.ShapeDtypeStruct((B, H, hd), F32),
        grid_spec=grid_spec,
        compiler_params=_params("parallel", "arbitrary"),
        name="fox_decode",
    )(page_table.reshape(-1), q, *([cache_k] * npg), *([cache_v] * npg), *([logf_t] * npg),
      k_new, v_new, lf_new, later)


def _first_argmax(x, n):
    io = lax.broadcasted_iota(jnp.int32, x.shape, 0)
    m = jnp.max(x, axis=0, keepdims=True)
    i = jnp.min(jnp.where(x == m, io, n), axis=0, keepdims=True)
    return m, i, io


def _stack_rows(rows, dtype):
    n = len(rows)
    io = lax.broadcasted_iota(jnp.int32, (n, rows[0].shape[1]), 0)
    out = jnp.broadcast_to(rows[0], io.shape).astype(dtype)
    for r in range(1, n):
        out = jnp.where(io == r, rows[r].astype(dtype), out)
    return out


def _route_body(h_ref, whi_ref, wlo_ref, b_ref, before_ref, cnt0_ref, idx_ref, w_ref, rank_ref, cnt_ref, carry_s):
    @pl.when(pl.program_id(0) == 0)
    def _init():
        carry_s[...] = cnt0_ref[...]

    h = h_ref[...]
    h_hi = h.astype(BF16)
    h_lo = (h - h_hi.astype(F32)).astype(BF16)
    whi = whi_ref[...]
    logits = _dot_nt(whi, h_hi) + _dot_nt(whi, h_lo) + _dot_nt(wlo_ref[...], h_hi)
    s = jax.nn.sigmoid(logits)
    sb = s + b_ref[...]
    E = sb.shape[0]
    gsz = E // N_GROUPS
    neg = -jnp.inf

    gscore = []
    for g in range(N_GROUPS):
        x = sb[g * gsz:(g + 1) * gsz]
        m1, i1, io = _first_argmax(x, gsz)
        m2 = jnp.max(jnp.where(io == i1, neg, x), axis=0, keepdims=True)
        gscore.append(m1 + m2)
    cur = _stack_rows(gscore, F32)
    keep = jnp.zeros(cur.shape, F32)
    for _ in range(TOPK_GROUPS):
        _, ig, iog = _first_argmax(cur, N_GROUPS)
        hit = iog == ig
        keep = jnp.where(hit, 1.0, keep)
        cur = jnp.where(hit, neg, cur)
    blocks = []
    for g in range(N_GROUPS):
        keep_g = jnp.max(jnp.where(iog == g, keep, 0.0), axis=0, keepdims=True)
        blocks.append(jnp.where(keep_g > 0.0, sb[g * gsz:(g + 1) * gsz], neg))
    cand = jnp.concatenate(blocks, axis=0)

    sel = jnp.zeros(cand.shape, F32)
    idxs, ws = [], []
    for _ in range(TOP_K):
        _, ik, ioe = _first_argmax(cand, E)
        hit = ioe == ik
        idxs.append(ik)
        ws.append(jnp.sum(jnp.where(hit, s, 0.0), axis=0, keepdims=True))
        cand = jnp.where(hit, neg, cand)
        sel = jnp.where(hit, 1.0, sel)
    wsum = ws[0]
    for k in range(1, TOP_K):
        wsum = wsum + ws[k]
    ws = [w / wsum * ROUTED_SCALE for w in ws]

    prefix = _dot(sel.astype(BF16), before_ref[...]) + carry_s[...]
    ranks = [jnp.sum(jnp.where(ioe == ik, prefix, 0.0), axis=0, keepdims=True) for ik in idxs]
    carry_s[...] = carry_s[...] + jnp.sum(sel, axis=1, keepdims=True)
    cnt_ref[...] = carry_s[...]
    idx_ref[...] = _stack_rows(idxs, jnp.int32)
    w_ref[...] = _stack_rows(ws, F32)
    rank_ref[...] = _stack_rows(ranks, F32).astype(jnp.int32)


def route_tokens(h, w_hi_t, w_lo_t, b_col, counts_in, tm=640):
    T, D = h.shape
    E = w_hi_t.shape[0]
    tm = _row_tile(T, tm)
    tok = lax.broadcasted_iota(jnp.int32, (tm, tm), 0)
    before = (tok < tok.T).astype(BF16)
    kt = lambda i: (0, i)
    const = lambda i: (0, 0)
    return pl.pallas_call(
        _route_body,
        out_shape=[jax.ShapeDtypeStruct((TOP_K, T), jnp.int32), jax.ShapeDtypeStruct((TOP_K, T), F32),
                   jax.ShapeDtypeStruct((TOP_K, T), jnp.int32), jax.ShapeDtypeStruct((E, 1), F32)],
        grid=(T // tm,),
        in_specs=[pl.BlockSpec((tm, D), lambda i: (i, 0)),
                  pl.BlockSpec((E, D), const), pl.BlockSpec((E, D), const),
                  pl.BlockSpec((E, 1), const), pl.BlockSpec((tm, tm), const), pl.BlockSpec((E, 1), const)],
        out_specs=[pl.BlockSpec((TOP_K, tm), kt), pl.BlockSpec((TOP_K, tm), kt),
                   pl.BlockSpec((TOP_K, tm), kt), pl.BlockSpec((E, 1), const)],
        scratch_shapes=[pltpu.VMEM((E, 1), F32)],
        compiler_params=_params("arbitrary"),
        name="route_tokens",
    )(h, w_hi_t, w_lo_t, b_col, before, counts_in)


def _slot(ps_ref, idx_ref, rank_ref, k, t):
    return ps_ref[idx_ref[k, t]] + rank_ref[k, t]


def _dispatch_body(ps_ref, idx_ref, rank_ref, h_ref, xs_in_ref, xs_ref, sem):
    del xs_in_ref
    tq = h_ref.shape[0]

    def row_copy(t, k):
        return pltpu.make_async_copy(h_ref.at[pl.ds(t, 1)],
                                     xs_ref.at[pl.ds(_slot(ps_ref, idx_ref, rank_ref, k, t), 1)], sem)

    def start(t, c):
        for k in range(TOP_K):
            row_copy(t, k).start()
        return c

    def wait(t, c):
        for k in range(TOP_K):
            row_copy(t, k).wait()
        return c

    lax.fori_loop(0, tq, start, 0)
    lax.fori_loop(0, tq, wait, 0)


def moe_dispatch(pad_start, idx, rank, h_packed, xs):
    T, Dp = h_packed.shape
    tq = MOE_DMA_TOKENS
    tq = _row_tile(T, tq)
    smem = lambda: pl.BlockSpec((TOP_K, tq), lambda i, ps: (0, i), memory_space=pltpu.SMEM)
    grid_spec = pltpu.PrefetchScalarGridSpec(
        num_scalar_prefetch=1,
        grid=(T // tq,),
        in_specs=[smem(), smem(),
                  pl.BlockSpec((tq, Dp), lambda i, ps: (i, 0)),
                  pl.BlockSpec(memory_space=pl.ANY)],
        out_specs=pl.BlockSpec(memory_space=pl.ANY),
        scratch_shapes=[pltpu.SemaphoreType.DMA(())])
    return pl.pallas_call(
        _dispatch_body,
        out_shape=jax.ShapeDtypeStruct(xs.shape, xs.dtype),
        grid_spec=grid_spec,
        input_output_aliases={4: 0},
        compiler_params=_params("arbitrary"),
        name="moe_dispatch",
    )(pad_start, idx, rank, h_packed, xs)


def _combine_body(ps_ref, idx_ref, rank_ref, w_ref, yb_ref, o_ref, buf, sem):
    tq = o_ref.shape[0]

    def row_copy(t, k):
        return pltpu.make_async_copy(yb_ref.at[pl.ds(_slot(ps_ref, idx_ref, rank_ref, k, t), 1)],
                                     buf.at[k, pl.ds(t, 1)], sem)

    def start(t, c):
        for k in range(TOP_K):
            row_copy(t, k).start()
        return c

    def wait(t, c):
        for k in range(TOP_K):
            row_copy(t, k).wait()
        return c

    lax.fori_loop(0, tq, start, 0)
    lax.fori_loop(0, tq, wait, 0)
    acc = w_ref[:, 0:1] * buf[0]
    for k in range(1, TOP_K):
        acc = acc + w_ref[:, k:k + 1] * buf[k]
    o_ref[...] = acc


def moe_combine(pad_start, idx, rank, w_col, yb):
    T = w_col.shape[0]
    D = yb.shape[1]
    tq = _row_tile(T, MOE_DMA_TOKENS)
    smem = lambda: pl.BlockSpec((TOP_K, tq), lambda i, ps: (0, i), memory_space=pltpu.SMEM)
    grid_spec = pltpu.PrefetchScalarGridSpec(
        num_scalar_prefetch=1,
        grid=(T // tq,),
        in_specs=[smem(), smem(),
                  pl.BlockSpec((tq, TOP_K), lambda i, ps: (i, 0)),
                  pl.BlockSpec(memory_space=pl.ANY)],
        out_specs=pl.BlockSpec((tq, D), lambda i, ps: (i, 0)),
        scratch_shapes=[pltpu.VMEM((TOP_K, tq, D), F32), pltpu.SemaphoreType.DMA(())])
    return pl.pallas_call(
        _combine_body,
        out_shape=jax.ShapeDtypeStruct((T, D), F32),
        grid_spec=grid_spec,
        compiler_params=_params("arbitrary"),
        name="moe_combine",
    )(pad_start, idx, rank, w_col, yb)


def _moe_body(seq_ref, tj_ref, tf_ref, na_ref, nv_ref, x_ref, wg_ref, wu_ref, wd_ref, y_ref,
              g_buf, u_buf, d_buf, wg_s, wu_s, wd_s, sem):
    i = pl.program_id(0)

    def fetch(j):
        e = seq_ref[j]
        slot = j % 2
        return [pltpu.make_async_copy(w.at[e], buf.at[slot], sem.at[slot, n])
                for n, (w, buf) in enumerate(((wg_ref, g_buf), (wu_ref, u_buf), (wd_ref, d_buf)))]

    @pl.when(i == 0)
    def _prime():
        for c in fetch(0):
            c.start()

        @pl.when(na_ref[0] > 1)
        def _second():
            for c in fetch(1):
                c.start()

    @pl.when(i >= nv_ref[0])
    def _unused_tile():
        y_ref[...] = jnp.zeros_like(y_ref)

    @pl.when(jnp.logical_and(i < nv_ref[0], tf_ref[i] == 1))
    def _load_expert():
        j = tj_ref[i]
        slot = j % 2
        for c in fetch(j):
            c.wait()
        wg_s[...] = g_buf[slot].astype(BF16)
        wu_s[...] = u_buf[slot].astype(BF16)
        wd_s[...] = d_buf[slot].astype(BF16)

        @pl.when(j + 2 < na_ref[0])
        def _refill():
            for c in fetch(j + 2):
                c.start()

    @pl.when(i < nv_ref[0])
    def _tile():
        xw = x_ref[...]
        half = xw.shape[1]
        x_lo = lax.bitcast_convert_type(xw << 16, F32).astype(BF16)
        x_hi = lax.bitcast_convert_type(xw & jnp.int32(-65536), F32).astype(BF16)
        g = _dot(x_lo, wg_s[0:half]) + _dot(x_hi, wg_s[half:2 * half])
        u = _dot(x_lo, wu_s[0:half]) + _dot(x_hi, wu_s[half:2 * half])
        y_ref[...] = _dot((jax.nn.silu(g) * u).astype(BF16), wd_s[...])


def moe_experts(xs, expert_seq, tile_seq, tile_first, n_active, n_valid, w_gate, w_up, w_down):
    R, Dp = xs.shape
    D = 2 * Dp
    tm = MOE_TILE
    nt = R // tm
    De = w_gate.shape[2]
    grid_spec = pltpu.PrefetchScalarGridSpec(
        num_scalar_prefetch=5,
        grid=(nt,),
        in_specs=[pl.BlockSpec((tm, Dp), lambda i, *_: (i, 0)),
                  pl.BlockSpec(memory_space=pl.ANY),
                  pl.BlockSpec(memory_space=pl.ANY),
                  pl.BlockSpec(memory_space=pl.ANY)],
        out_specs=pl.BlockSpec((tm, D), lambda i, *_: (i, 0)),
        scratch_shapes=[pltpu.VMEM((2, D, De), F32), pltpu.VMEM((2, D, De), F32), pltpu.VMEM((2, De, D), F32),
                        pltpu.VMEM((D, De), BF16), pltpu.VMEM((D, De), BF16), pltpu.VMEM((De, D), BF16),
                        pltpu.SemaphoreType.DMA((2, 3))])
    return pl.pallas_call(
        _moe_body,
        out_shape=jax.ShapeDtypeStruct((R, D), F32),
        grid_spec=grid_spec,
        compiler_params=_params("arbitrary"),
        name="moe_experts",
    )(expert_seq, tile_seq, tile_first, n_active, n_valid, xs, w_gate, w_up, w_down)


def routed_moe(groups, w_r_hi_t, w_r_lo_t, b_router, w_gate, w_up, w_down):
    E = N_EXPERTS
    tm = MOE_TILE
    b_col = b_router.astype(F32).reshape(E, 1)
    cnt = jnp.zeros((E, 1), F32)
    routes = []
    for h, _ in groups:
        idx, w, rank, cnt = route_tokens(h, w_r_hi_t, w_r_lo_t, b_col, cnt)
        routes.append((idx, w, rank))

    counts = cnt[:, 0].astype(jnp.int32)
    padded = (counts + tm - 1) // tm * tm
    pad_end = jnp.cumsum(padded)
    pad_start = pad_end - padded
    n_tokens = sum(h.shape[0] for h, _ in groups)
    nt = -(-(n_tokens * TOP_K) // tm) + E
    n_valid = (pad_end[-1] // tm).astype(jnp.int32)
    tile_row = jnp.minimum(jnp.arange(nt, dtype=jnp.int32), n_valid - 1)
    tile_expert = jnp.minimum(jnp.sum((pad_end[None, :] <= (tile_row * tm)[:, None]).astype(jnp.int32), axis=1),
                              E - 1)
    tile_first = jnp.concatenate([jnp.ones((1,), jnp.int32),
                                  (tile_expert[1:] != tile_expert[:-1]).astype(jnp.int32)])
    seq_of = jnp.cumsum((counts > 0).astype(jnp.int32)) - 1
    n_active = seq_of[-1] + 1
    expert_seq = jnp.minimum(jnp.sum((seq_of[None, :] < jnp.arange(E, dtype=jnp.int32)[:, None]).astype(jnp.int32),
                                     axis=1), E - 1)
    tile_seq = seq_of[tile_expert]

    xs = jnp.zeros((nt * tm, groups[0][1].shape[1]), jnp.int32)
    for (_, h_packed), (idx, _, rank) in zip(groups, routes):
        xs = moe_dispatch(pad_start, idx, rank, h_packed, xs)
    yb = moe_experts(xs, expert_seq, tile_seq, tile_first, n_active.reshape(1), n_valid.reshape(1),
                     w_gate, w_up, w_down)
    return [moe_combine(pad_start, idx, rank, w.T, yb) for idx, w, rank in routes]


def _prep_weights(w_in, b_ml_i, b_ml_f, b_fx_f, w_br_ml, w_br_fx, w_out, w_router,
                  w_sh_gate, w_sh_up, w_sh_down, w_ple_gate, w_ple_proj):
    o_ml = 2 * ML_W_QK + 2 * ML_W_V
    o_fx = o_ml + 2 * ML_HEADS
    o_ff = o_fx + 3 * FX_W
    o_g = o_ff + FX_HEADS
    wb = w_in.astype(BF16)
    small = jnp.concatenate([wb[:, o_ml:o_fx], wb[:, o_ff:o_g]], axis=1)
    small = jnp.pad(small, ((0, 0), (0, LANES - small.shape[1])))
    w_r_hi = _round_bf16(w_router.T)
    w_r_lo = (w_router.T - w_r_hi).astype(BF16)
    w_r_hi = w_r_hi.astype(BF16)
    return dict(
        w_ml=wb[:, :o_ml], w_fq=wb[:, o_fx:o_fx + FX_W], w_fk=wb[:, o_fx + FX_W:o_fx + 2 * FX_W],
        w_fv=wb[:, o_fx + 2 * FX_W:o_ff], w_g=wb[:, o_g:], w_small=small,
        w_br_ml=w_br_ml.astype(BF16), w_br_fx=w_br_fx.astype(BF16), w_out=w_out.astype(BF16),
        w_r_hi=w_r_hi, w_r_lo=w_r_lo,
        w_sh_gate=w_sh_gate.astype(BF16), w_sh_up=w_sh_up.astype(BF16), w_sh_down=w_sh_down.astype(BF16),
        w_ple_gate=w_ple_gate.astype(BF16), w_ple_proj=w_ple_proj.astype(BF16))


def _in_proj(x_bf, W):
    (z_ml,) = matmul(x_bf, W['w_ml'], name="proj_ml")
    (fq,) = matmul(x_bf, W['w_fq'], out_dtypes=(BF16,), scale=FX_SCALE, name="proj_fq")
    fk, fk_bf = matmul(x_bf, W['w_fk'], out_dtypes=(F32, BF16), name="proj_fk")
    fv, fv_bf = matmul(x_bf, W['w_fv'], out_dtypes=(F32, BF16), name="proj_fv")
    (z_g,) = matmul(x_bf, W['w_g'], name="proj_gates")
    (z_s,) = matmul(x_bf, W['w_small'], name="proj_small")
    return z_ml, fq, fk, fk_bf, fv, fv_bf, z_g, z_s


def _small_gates(z_s, b_ml_i, b_ml_f, b_fx_f):
    H = ML_HEADS
    li = z_s[:, 0:H] + b_ml_i
    lf = jax.nn.log_sigmoid(z_s[:, H:2 * H] + b_ml_f)
    lf_fx = jax.nn.log_sigmoid(z_s[:, 2 * H:2 * H + FX_HEADS] + b_fx_f)
    return li, lf, lf_fx


def kernel(x_prompt, x_sample, p_prompt, p_sample, cache_k, cache_v, cache_logf, state_C, state_n, state_m,
           page_table, w_in, b_ml_i, b_ml_f, b_fx_f, ml_norm_g, w_br_ml, w_br_fx, w_out, ln1_g, ln1_b,
           w_router, b_router, w_exp_gate, w_exp_up, w_exp_down, w_sh_gate, w_sh_up, w_sh_down,
           w_ple_gate, w_ple_proj, ln2_g, ln2_b):
    T = x_prompt.shape[1]
    B = x_sample.shape[0]
    H = ML_HEADS
    W = _prep_weights(w_in[0], b_ml_i[0], b_ml_f[0], b_fx_f[0], w_br_ml[0], w_br_fx[0], w_out[0], w_router[0],
                      w_sh_gate[0], w_sh_up[0], w_sh_down[0], w_ple_gate[0], w_ple_proj[0])
    xp = x_prompt[0]
    xs = x_sample[:, 0]

    z_ml, fq, fk, fk_bf, fv, fv_bf, z_g, z_s = _in_proj(xp.astype(BF16), W)
    li, lf, lf_fx = _small_gates(z_s, b_ml_i[0], b_ml_f[0], b_fx_f[0])
    nc = T // ML_CHUNK
    to_rows = lambda a: a.T.reshape(H, nc, 1, ML_CHUNK)
    hm_p, C_p, n_p, m_p = mlstm_prompt(z_ml, to_rows(li), to_rows(lf), ml_norm_g)
    hfx_p = fox_prompt(fq, fk_bf, fv_bf, -jnp.cumsum(lf_fx, axis=0))
    merged_p = gated_merge(hm_p, W['w_br_ml'], hfx_p, W['w_br_fx'], z_g, BF16)
    h1_p, h1b_p, h1k_p = out_proj_ln(merged_p, W['w_out'], xp, ln1_g, ln1_b)

    zd_ml, dq, dk, dk_bf, dv, dv_bf, zd_g, zd_s = _in_proj(xs.astype(BF16), W)
    dli, dlf, dlf_fx = _small_gates(zd_s, b_ml_i[0], b_ml_f[0], b_fx_f[0])
    col = lambda a: a.reshape(B, H, ML_DQK, 1)
    gates = jnp.stack([dli, dlf, state_m[0]], axis=-1)
    gates = jnp.pad(gates, ((0, 0), (0, 0), (0, LANES - 3))).reshape(B, H, 1, LANES)
    hm_d, C_d, n_d, m_d = mlstm_decode(
        state_C[0], col(state_n[0]), col(zd_ml[:, :ML_W_QK]), col(zd_ml[:, ML_W_QK:2 * ML_W_QK]),
        zd_ml[:, 2 * ML_W_QK:2 * ML_W_QK + ML_W_V].reshape(B, H, 1, ML_DV),
        zd_ml[:, 2 * ML_W_QK + ML_W_V:].reshape(B, H, 1, ML_DV),
        gates, ml_norm_g.reshape(H, 1, ML_DV))
    n_pool = cache_k.shape[1]
    page_rows = PAGE_SIZE * FX_HEADS
    row_of = jnp.arange(page_rows, dtype=jnp.int32) // FX_HEADS
    later = (jnp.arange(PAGE_SIZE, dtype=jnp.int32)[None, :] > row_of[:, None]).astype(BF16)
    heads = lambda a: a.reshape(B, FX_HEADS, FX_HD)
    hfx_d = fox_decode(page_table, heads(dq), cache_k.reshape(n_pool, page_rows, FX_HD),
                       cache_v.reshape(n_pool, page_rows, FX_HD), jnp.swapaxes(cache_logf[0], 1, 2),
                       heads(dk), heads(dv), dlf_fx[:, :, None], later)
    merged_d = gated_merge(hm_d.reshape(B, ML_W_V), W['w_br_ml'], hfx_d.reshape(B, FX_W).astype(BF16),
                           W['w_br_fx'], zd_g, BF16)
    h1_d, h1b_d, h1k_d = out_proj_ln(merged_d, W['w_out'], xs, ln1_g, ln1_b)

    moe_p, moe_d = routed_moe([(h1_p, h1k_p), (h1_d, h1k_d)], W['w_r_hi'], W['w_r_lo'], b_router[0],
                              w_exp_gate[0], w_exp_up[0], w_exp_down[0])
    ple_p = ple_embed(h1b_p, W['w_ple_gate'], p_prompt[0, 0].astype(BF16), W['w_ple_proj'])
    ple_d = ple_embed(h1b_d, W['w_ple_gate'], p_sample[0, :, 0].astype(BF16), W['w_ple_proj'])
    y_p = ffn_tail(h1b_p, h1_p, moe_p, ple_p, W['w_sh_gate'], W['w_sh_up'], W['w_sh_down'], ln2_g, ln2_b)
    y_d = ffn_tail(h1b_d, h1_d, moe_d, ple_d, W['w_sh_gate'], W['w_sh_up'], W['w_sh_down'], ln2_g, ln2_b)

    return (y_p[None], y_d[:, None],
            fk.reshape(1, 1, T, FX_HEADS, FX_HD), fv.reshape(1, 1, T, FX_HEADS, FX_HD), lf_fx[None, None],
            C_p[None, None], n_p.reshape(1, 1, H, ML_DQK), m_p[:, 0, 0].reshape(1, 1, H),
            dk.reshape(1, B, 1, FX_HEADS, FX_HD), dv.reshape(1, B, 1, FX_HEADS, FX_HD), dlf_fx[None, :, None],
            C_d[None], n_d.reshape(1, B, H, ML_DQK), m_d[:, :, 0, 0][None])
```

```python
import functools

import jax
import jax.numpy as jnp
from jax import lax
from jax.experimental import pallas as pl
from jax.experimental.pallas import tpu as pltpu

F32 = jnp.float32
BF16 = jnp.bfloat16

D_MODEL = 2048
ML_HEADS = 4
ML_DQK = 256
ML_DV = 512
ML_CHUNK = 128
ML_W_QK = ML_HEADS * ML_DQK
ML_W_V = ML_HEADS * ML_DV
FX_HEADS = 16
FX_HD = 128
FX_W = FX_HEADS * FX_HD
FX_SCALE = FX_HD ** -0.5
PAGE_SIZE = 128
N_EXPERTS = 256
TOP_K = 8
N_GROUPS = 8
TOPK_GROUPS = 4
D_EXPERT = 512
ROUTED_SCALE = 2.5
LN_EPS = 1e-5
DEEPNORM_ALPHA = 2.0 ** 0.25

LANES = 128
SUBLANES = 8
VMEM_LIMIT = 56 * 1024 * 1024
MOE_TILE = 320
MOE_DMA_TOKENS = 128
FOX_BLOCK = 1024
FOX_DECODE_PAGES = (4, 2, 1)


def _params(*sem):
    return pltpu.CompilerParams(dimension_semantics=sem, vmem_limit_bytes=VMEM_LIMIT)


def _row_tile(m, target):
    if m <= target:
        return m
    best = max(t for t in range(LANES, target + 1, LANES) if m % t == 0)
    assert m % best == 0
    return best


def _dot(a, b):
    return jnp.dot(a, b, preferred_element_type=F32)


def _dot_nt(a, b):
    return lax.dot_general(a, b, (((1,), (1,)), ((), ())), preferred_element_type=F32)


def _split3(x):
    hi = x.astype(BF16)
    r1 = x - hi.astype(F32)
    mid = r1.astype(BF16)
    lo = (r1 - mid.astype(F32)).astype(BF16)
    return hi, mid, lo


def _round_bf16(x):
    return lax.reduce_precision(x, exponent_bits=8, mantissa_bits=7)


def _split3_traced(x):
    hi = _round_bf16(x)
    mid = _round_bf16(x - hi)
    lo = _round_bf16(x - hi - mid)
    return hi.astype(BF16), mid.astype(BF16), lo.astype(BF16)


def _layer_norm(y, g, b):
    mu = jnp.mean(y, axis=-1, keepdims=True)
    d = y - mu
    var = jnp.mean(d * d, axis=-1, keepdims=True)
    return d * lax.rsqrt(var + LN_EPS) * g + b


def _mm_body(a_ref, b_ref, *o_refs, scale):
    acc = _dot(a_ref[...], b_ref[...])
    if scale != 1.0:
        acc = acc * scale
    for o in o_refs:
        o[...] = acc.astype(o.dtype)


def matmul(a, b, out_dtypes=(F32,), tm=512, tn=1024, scale=1.0, name="matmul"):
    M, K = a.shape
    N = b.shape[1]
    tm, tn = _row_tile(M, tm), _row_tile(N, tn)
    outs = pl.pallas_call(
        functools.partial(_mm_body, scale=scale),
        out_shape=[jax.ShapeDtypeStruct((M, N), d) for d in out_dtypes],
        grid=(M // tm, N // tn),
        in_specs=[pl.BlockSpec((tm, K), lambda i, j: (i, 0)),
                  pl.BlockSpec((K, tn), lambda i, j: (0, j))],
        out_specs=[pl.BlockSpec((tm, tn), lambda i, j: (i, j)) for _ in out_dtypes],
        compiler_params=_params("parallel", "parallel"),
        name=name,
    )(a, b)
    return outs


def _pair_body(a1_ref, b1_ref, a2_ref, b2_ref, g1_ref, g2_ref, o_ref, *, mode):
    y1 = _dot(a1_ref[...], b1_ref[...])
    y2 = _dot(a2_ref[...], b2_ref[...])
    if mode == "merge":
        out = jax.nn.sigmoid(g1_ref[...]) * y1 + jax.nn.sigmoid(g2_ref[...]) * y2
    else:
        out = jax.nn.sigmoid(y1) * y2
    o_ref[...] = out.astype(o_ref.dtype)


def gated_merge(a1, b1, a2, b2, gates, out_dtype, tm=512, tn=512):
    M, K = a1.shape
    N = b1.shape[1]
    tm, tn = _row_tile(M, tm), _row_tile(N, tn)
    nj = N // tn
    return pl.pallas_call(
        functools.partial(_pair_body, mode="merge"),
        out_shape=jax.ShapeDtypeStruct((M, N), out_dtype),
        grid=(M // tm, nj),
        in_specs=[pl.BlockSpec((tm, K), lambda i, j: (i, 0)),
                  pl.BlockSpec((K, tn), lambda i, j: (0, j)),
                  pl.BlockSpec((tm, K), lambda i, j: (i, 0)),
                  pl.BlockSpec((K, tn), lambda i, j: (0, j)),
                  pl.BlockSpec((tm, tn), lambda i, j: (i, j)),
                  pl.BlockSpec((tm, tn), lambda i, j: (i, j + nj))],
        out_specs=pl.BlockSpec((tm, tn), lambda i, j: (i, j)),
        compiler_params=_params("parallel", "parallel"),
        name="gated_merge",
    )(a1, b1, a2, b2, gates, gates)


def _ple_body(a1_ref, b1_ref, a2_ref, b2_ref, o_ref):
    y1 = _dot(a1_ref[...], b1_ref[...])
    y2 = _dot(a2_ref[...], b2_ref[...])
    o_ref[...] = (jax.nn.sigmoid(y1) * y2).astype(o_ref.dtype)


def ple_embed(h, w_gate, p, w_proj, tm=512, tn=512):
    M, K1 = h.shape
    K2 = p.shape[1]
    N = w_gate.shape[1]
    tm, tn = _row_tile(M, tm), _row_tile(N, tn)
    return pl.pallas_call(
        _ple_body,
        out_shape=jax.ShapeDtypeStruct((M, N), F32),
        grid=(M // tm, N // tn),
        in_specs=[pl.BlockSpec((tm, K1), lambda i, j: (i, 0)),
                  pl.BlockSpec((K1, tn), lambda i, j: (0, j)),
                  pl.BlockSpec((tm, K2), lambda i, j: (i, 0)),
                  pl.BlockSpec((K2, tn), lambda i, j: (0, j))],
        out_specs=pl.BlockSpec((tm, tn), lambda i, j: (i, j)),
        compiler_params=_params("parallel", "parallel"),
        name="ple_embed",
    )(h, w_gate, p, w_proj)


def _outln_body(mg_ref, w_ref, x_ref, g_ref, b_ref, hf_ref, hb_ref, hp_ref):
    y = DEEPNORM_ALPHA * x_ref[...] + _dot(mg_ref[...], w_ref[...])
    h = _layer_norm(y, g_ref[...], b_ref[...])
    hf_ref[...] = h
    hb = h.astype(BF16)
    hb_ref[...] = hb
    half = h.shape[1] // 2
    bits = lax.bitcast_convert_type(hb.astype(F32), jnp.int32)
    hp_ref[...] = lax.shift_right_logical(bits[:, :half], 16) | (bits[:, half:] & jnp.int32(-65536))


def out_proj_ln(merged, w_out, x, g, b, tm=256):
    M, K = merged.shape
    N = w_out.shape[1]
    tm = _row_tile(M, tm)
    return pl.pallas_call(
        _outln_body,
        out_shape=[jax.ShapeDtypeStruct((M, N), F32), jax.ShapeDtypeStruct((M, N), BF16),
                   jax.ShapeDtypeStruct((M, N // 2), jnp.int32)],
        grid=(M // tm,),
        in_specs=[pl.BlockSpec((tm, K), lambda i: (i, 0)),
                  pl.BlockSpec((K, N), lambda i: (0, 0)),
                  pl.BlockSpec((tm, N), lambda i: (i, 0)),
                  pl.BlockSpec((1, N), lambda i: (0, 0)),
                  pl.BlockSpec((1, N), lambda i: (0, 0))],
        out_specs=[pl.BlockSpec((tm, N), lambda i: (i, 0)),
                   pl.BlockSpec((tm, N), lambda i: (i, 0)),
                   pl.BlockSpec((tm, N // 2), lambda i: (i, 0))],
        compiler_params=_params("parallel"),
        name="out_proj_ln",
    )(merged, w_out, x, g, b)


def _ffn_tail_body(hb_ref, hf_ref, moe_ref, ple_ref, wg_ref, wu_ref, wd_ref, g_ref, b_ref, y_ref):
    hb = hb_ref[...]
    sg = _dot(hb, wg_ref[...])
    su = _dot(hb, wu_ref[...])
    sh = _dot((jax.nn.silu(sg) * su).astype(BF16), wd_ref[...])
    y = DEEPNORM_ALPHA * hf_ref[...] + moe_ref[...] + sh + ple_ref[...]
    y_ref[...] = _layer_norm(y, g_ref[...], b_ref[...])


def ffn_tail(h_bf, h_f32, moe, ple, wg, wu, wd, g, b, tm=256):
    M, D = h_bf.shape
    Ds = wg.shape[1]
    tm = _row_tile(M, tm)
    row = lambda i: (i, 0)
    const = lambda i: (0, 0)
    return pl.pallas_call(
        _ffn_tail_body,
        out_shape=jax.ShapeDtypeStruct((M, D), F32),
        grid=(M // tm,),
        in_specs=[pl.BlockSpec((tm, D), row), pl.BlockSpec((tm, D), row),
                  pl.BlockSpec((tm, D), row), pl.BlockSpec((tm, D), row),
                  pl.BlockSpec((D, Ds), const), pl.BlockSpec((D, Ds), const),
                  pl.BlockSpec((Ds, D), const),
                  pl.BlockSpec((1, D), const), pl.BlockSpec((1, D), const)],
        out_specs=pl.BlockSpec((tm, D), row),
        compiler_params=_params("parallel"),
        name="ffn_tail",
    )(h_bf, h_f32, moe, ple, wg, wu, wd, g, b)


def _mlstm_prompt_body(q_ref, k_ref, v_ref, o_ref, li_ref, lf_ref, g_ref,
                       hm_ref, c_out, n_out, m_out, c_s, n_s, m_s):
    c = pl.program_id(1)
    L = ML_CHUNK

    @pl.when(c == 0)
    def _init():
        c_s[...] = jnp.zeros_like(c_s)
        n_s[...] = jnp.zeros_like(n_s)
        m_s[...] = jnp.zeros_like(m_s)

    q = q_ref[...]
    k = k_ref[...] * (ML_DQK ** -0.5)
    v_bf = v_ref[...].astype(BF16)
    q_bf = q.astype(BF16)
    li_r = li_ref[0, 0]
    lf_r = lf_ref[0, 0]

    row = lax.broadcasted_iota(jnp.int32, (L, L), 0)
    col = lax.broadcasted_iota(jnp.int32, (L, L), 1)
    eye = row == col
    causal = col <= row
    lf_c = jnp.sum(jnp.where(eye, lf_r, 0.0), axis=1, keepdims=True)
    b_c = jnp.sum(jnp.where(causal, lf_r, 0.0), axis=1, keepdims=True)
    b_r = jnp.sum(jnp.where(row <= col, lf_c, 0.0), axis=0, keepdims=True)
    b_tot = jnp.sum(lf_r, axis=1, keepdims=True)

    m_prev = m_s[:, 0:1]
    dlog = jnp.where(causal, b_c - b_r + li_r, -jnp.inf)
    inter = b_c + m_prev
    m_row = jnp.maximum(inter, jnp.max(dlog, axis=1, keepdims=True))
    w_inter = jnp.exp(inter - m_row)
    s = _dot_nt(q_bf, k.astype(BF16)) * jnp.exp(dlog - m_row)
    num = w_inter * _dot(q_bf, c_s[...].astype(BF16)) + _dot(s.astype(BF16), v_bf)
    den = w_inter * jnp.sum(q * n_s[...], axis=1, keepdims=True) + jnp.sum(s, axis=1, keepdims=True)
    h = num / jnp.maximum(jnp.abs(den), jnp.exp(-m_row))

    hn = h * lax.rsqrt(jnp.mean(h * h, axis=1, keepdims=True) + LN_EPS)
    hm_ref[...] = (hn * g_ref[...] * jax.nn.sigmoid(o_ref[...])).astype(hm_ref.dtype)

    lw_r = b_tot - b_r + li_r
    m_new = jnp.maximum(b_tot + m_prev, jnp.max(lw_r, axis=1, keepdims=True))
    decay = jnp.exp(b_tot + m_prev - m_new)
    wgt_r = jnp.exp(lw_r - m_new)
    wgt_c = jnp.sum(jnp.where(eye, wgt_r, 0.0), axis=1, keepdims=True)
    wk = k * wgt_c
    c_new = decay * c_s[...] + _dot(wk.T.astype(BF16), v_bf)
    n_new = decay * n_s[...] + jnp.sum(wk, axis=0, keepdims=True)
    c_s[...] = c_new
    n_s[...] = n_new
    m_s[...] = jnp.broadcast_to(m_new, m_s.shape)

    @pl.when(c == pl.num_programs(1) - 1)
    def _final():
        c_out[0] = c_new
        n_out[0] = n_new
        m_out[0] = jnp.broadcast_to(m_new, (1, LANES))


def mlstm_prompt(z, li_rows, lf_rows, norm_g):
    T = z.shape[0]
    nc = T // ML_CHUNK
    kq = ML_W_QK // ML_DQK
    kv = 2 * ML_W_QK // ML_DV
    ko = kv + ML_HEADS
    return pl.pallas_call(
        _mlstm_prompt_body,
        out_shape=[jax.ShapeDtypeStruct((T, ML_W_V), BF16),
                   jax.ShapeDtypeStruct((ML_HEADS, ML_DQK, ML_DV), F32),
                   jax.ShapeDtypeStruct((ML_HEADS, 1, ML_DQK), F32),
                   jax.ShapeDtypeStruct((ML_HEADS, 1, LANES), F32)],
        grid=(ML_HEADS, nc),
        in_specs=[pl.BlockSpec((ML_CHUNK, ML_DQK), lambda h, c: (c, h)),
                  pl.BlockSpec((ML_CHUNK, ML_DQK), lambda h, c: (c, kq + h)),
                  pl.BlockSpec((ML_CHUNK, ML_DV), lambda h, c: (c, kv + h)),
                  pl.BlockSpec((ML_CHUNK, ML_DV), lambda h, c: (c, ko + h)),
                  pl.BlockSpec((1, 1, 1, ML_CHUNK), lambda h, c: (h, c, 0, 0)),
                  pl.BlockSpec((1, 1, 1, ML_CHUNK), lambda h, c: (h, c, 0, 0)),
                  pl.BlockSpec((1, ML_DV), lambda h, c: (0, h))],
        out_specs=[pl.BlockSpec((ML_CHUNK, ML_DV), lambda h, c: (c, h)),
                   pl.BlockSpec((1, ML_DQK, ML_DV), lambda h, c: (h, 0, 0)),
                   pl.BlockSpec((1, 1, ML_DQK), lambda h, c: (h, 0, 0)),
                   pl.BlockSpec((1, 1, LANES), lambda h, c: (h, 0, 0))],
        scratch_shapes=[pltpu.VMEM((ML_DQK, ML_DV), F32),
                        pltpu.VMEM((1, ML_DQK), F32),
                        pltpu.VMEM((1, LANES), F32)],
        compiler_params=_params("parallel", "arbitrary"),
        name="mlstm_prompt",
    )(z, z, z, z, li_rows, lf_rows, norm_g)


def _mlstm_decode_body(c_ref, n_ref, q_ref, k_ref, v_ref, o_ref, gate_ref, g_ref,
                       hm_ref, c_out, n_out, m_out):
    for h in range(ML_HEADS):
        C = c_ref[0, h]
        n_c = n_ref[0, h]
        q_c = q_ref[0, h]
        k_c = k_ref[0, h] * (ML_DQK ** -0.5)
        v_r = v_ref[0, h]
        gt = gate_ref[0, h]
        li, lf, m0 = gt[:, 0:1], gt[:, 1:2], gt[:, 2:3]
        inter = lf + m0
        m_new = jnp.maximum(inter, li)
        w_inter = jnp.exp(inter - m_new)
        w_in = jnp.exp(li - m_new)
        qk = jnp.sum(q_c * k_c, axis=0, keepdims=True) * w_in
        qc = jnp.sum(C * q_c, axis=0, keepdims=True)
        num = w_inter * qc + qk * v_r
        den = w_inter * jnp.sum(q_c * n_c, axis=0, keepdims=True) + qk
        hh = num / jnp.maximum(jnp.abs(den), jnp.exp(-m_new))
        hn = hh * lax.rsqrt(jnp.mean(hh * hh, axis=1, keepdims=True) + LN_EPS)
        hm_ref[0, h] = (hn * g_ref[h] * jax.nn.sigmoid(o_ref[0, h])).astype(hm_ref.dtype)
        wk = k_c * w_in
        c_out[0, h] = w_inter * C + wk * v_r
        n_out[0, h] = w_inter * n_c + wk
        m_out[0, h] = jnp.broadcast_to(m_new, (1, LANES))


def mlstm_decode(state_c, state_n_col, q_col, k_col, v_row, o_row, gates, norm_g):
    B = state_c.shape[0]
    H = ML_HEADS
    blk4 = lambda *s: pl.BlockSpec((1, H) + s, lambda b: (b, 0, 0, 0))
    return pl.pallas_call(
        _mlstm_decode_body,
        out_shape=[jax.ShapeDtypeStruct((B, H, 1, ML_DV), BF16),
                   jax.ShapeDtypeStruct((B, H, ML_DQK, ML_DV), F32),
                   jax.ShapeDtypeStruct((B, H, ML_DQK, 1), F32),
                   jax.ShapeDtypeStruct((B, H, 1, LANES), F32)],
        grid=(B,),
        in_specs=[blk4(ML_DQK, ML_DV), blk4(ML_DQK, 1), blk4(ML_DQK, 1), blk4(ML_DQK, 1),
                  blk4(1, ML_DV), blk4(1, ML_DV), blk4(1, LANES),
                  pl.BlockSpec((H, 1, ML_DV), lambda b: (0, 0, 0))],
        out_specs=[blk4(1, ML_DV), blk4(ML_DQK, ML_DV), blk4(ML_DQK, 1), blk4(1, LANES)],
        compiler_params=_params("parallel"),
        name="mlstm_decode",
    )(state_c, state_n_col, q_col, k_col, v_row, o_row, gates, norm_g)


def _fox_prompt_body(qi_tab, ki_tab, q_ref, k_ref, v_ref, kb_ref, ones_ref, o_ref, m_s, acc_s):
    p = pl.program_id(1)
    qi = qi_tab[p]
    ki = ki_tab[p]
    tb = q_ref.shape[0]
    hd = q_ref.shape[1]

    @pl.when(ki == 0)
    def _init():
        m_s[...] = jnp.full_like(m_s, -jnp.inf)
        acc_s[...] = jnp.zeros_like(acc_s)

    ones = ones_ref[...]
    t = _dot_nt(jnp.concatenate([q_ref[...], ones], axis=1),
                jnp.concatenate([k_ref[...], kb_ref[0]], axis=1))
    v_ext = jnp.concatenate([v_ref[...], ones], axis=1)

    def update(t):
        m_prev = m_s[...]
        m_new = jnp.maximum(m_prev, jnp.max(t, axis=1, keepdims=True))
        pr = jnp.exp(t - m_new)
        acc_s[...] = jnp.exp(m_prev - m_new) * acc_s[...] + _dot(pr.astype(BF16), v_ext)
        m_s[...] = m_new

    @pl.when(ki < qi)
    def _off_diag():
        update(t)

    @pl.when(ki == qi)
    def _diag():
        row = lax.broadcasted_iota(jnp.int32, (tb, tb), 0)
        col = lax.broadcasted_iota(jnp.int32, (tb, tb), 1)
        update(jnp.where(col <= row, t, -jnp.inf))
        acc = acc_s[...]
        o_ref[...] = (acc[:, 0:hd] / acc[:, hd:hd + 1]).astype(o_ref.dtype)


def fox_prompt(q, k, v, neg_cum, tb=FOX_BLOCK):
    T = q.shape[0]
    tb = min(tb, T)
    nb = T // tb
    pairs = [(i, j) for i in range(nb) for j in range(i + 1)]
    qi_tab = jnp.array([a for a, _ in pairs], jnp.int32)
    ki_tab = jnp.array([b for _, b in pairs], jnp.int32)
    n_split = 3
    kb = jnp.stack(_split3_traced(neg_cum.T), axis=-1)
    kb = jnp.pad(kb, ((0, 0), (0, 0), (0, FX_HD - n_split)))
    ones = (lax.broadcasted_iota(jnp.int32, (tb, FX_HD), 1) < n_split).astype(BF16)
    grid_spec = pltpu.PrefetchScalarGridSpec(
        num_scalar_prefetch=2,
        grid=(FX_HEADS, len(pairs)),
        in_specs=[pl.BlockSpec((tb, FX_HD), lambda h, p, qt, kt: (qt[p], h)),
                  pl.BlockSpec((tb, FX_HD), lambda h, p, qt, kt: (kt[p], h)),
                  pl.BlockSpec((tb, FX_HD), lambda h, p, qt, kt: (kt[p], h)),
                  pl.BlockSpec((1, tb, FX_HD), lambda h, p, qt, kt: (h, kt[p], 0)),
                  pl.BlockSpec((tb, FX_HD), lambda h, p, qt, kt: (0, 0))],
        out_specs=pl.BlockSpec((tb, FX_HD), lambda h, p, qt, kt: (qt[p], h)),
        scratch_shapes=[pltpu.VMEM((tb, 1), F32), pltpu.VMEM((tb, 2 * FX_HD), F32)])
    return pl.pallas_call(
        _fox_prompt_body,
        out_shape=jax.ShapeDtypeStruct((T, FX_W), BF16),
        grid_spec=grid_spec,
        compiler_params=_params("parallel", "arbitrary"),
        name="fox_prompt",
    )(qi_tab, ki_tab, q, k, v, kb, ones)


def _fox_decode_body(pt_ref, q_ref, *refs, pages_per_step):
    npg = pages_per_step
    kc_refs, vc_refs, lft_refs = refs[0:npg], refs[npg:2 * npg], refs[2 * npg:3 * npg]
    knew_ref, vnew_ref, lfnew_ref, later_ref, o_ref, m_s, l_s, acc_s, carry_s = refs[3 * npg:]
    pg = pl.program_id(1)
    H = FX_HEADS
    q = q_ref[0]

    @pl.when(pg == 0)
    def _init():
        m_s[...] = jnp.sum(q.astype(F32) * knew_ref[0], axis=1, keepdims=True)
        l_s[...] = jnp.ones_like(l_s)
        acc_s[...] = vnew_ref[0]
        carry_s[...] = jnp.zeros_like(carry_s)

    carry = carry_s[...]
    ts = []
    for kc_ref, lft_ref in zip(kc_refs, lft_refs):
        lpt = lft_ref[0]
        hi, mid, lo = _split3(lpt)
        decay = _dot_nt(jnp.concatenate([hi, mid, lo], axis=0), later_ref[...])
        t = (_dot_nt(q, kc_ref[0].astype(BF16)) + decay[0:H] + decay[H:2 * H] + decay[2 * H:3 * H]
             + (carry + lfnew_ref[0]))
        carry = carry + jnp.sum(lpt, axis=1, keepdims=True)
        sub = lax.broadcasted_iota(jnp.int32, t.shape, 0)
        lane = lax.broadcasted_iota(jnp.int32, t.shape, 1)
        ts.append(jnp.where((lane & (H - 1)) == sub, t, -jnp.inf))
    carry_s[...] = carry
    m_prev = m_s[...]
    m_new = m_prev
    for t in ts:
        m_new = jnp.maximum(m_new, jnp.max(t, axis=1, keepdims=True))
    alpha = jnp.exp(m_prev - m_new)
    l_new = alpha * l_s[...]
    acc = alpha * acc_s[...]
    for t, vc_ref in zip(ts, vc_refs):
        pr = jnp.exp(t - m_new)
        l_new = l_new + jnp.sum(pr, axis=1, keepdims=True)
        acc = acc + _dot(pr.astype(BF16), vc_ref[0].astype(BF16))
    l_s[...] = l_new
    acc_s[...] = acc
    m_s[...] = m_new

    @pl.when(pg == pl.num_programs(1) - 1)
    def _final():
        o_ref[0] = acc_s[...] / l_s[...]


def fox_decode(page_table, q, cache_k, cache_v, logf_t, k_new, v_new, lf_new, later):
    B, n_pages = page_table.shape
    H, hd = FX_HEADS, FX_HD
    assert H & (H - 1) == 0
    rows = PAGE_SIZE * H
    npg = max(n for n in FOX_DECODE_PAGES if n_pages % n == 0)
    page = lambda j: (lambda b, p, pt: (pt[b * n_pages + n_pages - 1 - (p * npg + j)], 0, 0))
    seq = lambda b, p, pt: (b, 0, 0)
    grid_spec = pltpu.PrefetchScalarGridSpec(
        num_scalar_prefetch=1,
        grid=(B, n_pages // npg),
        in_specs=([pl.BlockSpec((1, H, hd), seq)]
                  + [pl.BlockSpec((1, rows, hd), page(j)) for j in range(npg)]
                  + [pl.BlockSpec((1, rows, hd), page(j)) for j in range(npg)]
                  + [pl.BlockSpec((1, H, PAGE_SIZE), page(j)) for j in range(npg)]
                  + [pl.BlockSpec((1, H, hd), seq),
                     pl.BlockSpec((1, H, hd), seq),
                     pl.BlockSpec((1, H, 1), seq),
                     pl.BlockSpec((rows, PAGE_SIZE), lambda b, p, pt: (0, 0))]),
        out_specs=pl.BlockSpec((1, H, hd), seq),
        scratch_shapes=[pltpu.VMEM((H, 1), F32), pltpu.VMEM((H, 1), F32),
                        pltpu.VMEM((H, hd), F32), pltpu.VMEM((H, 1), F32)])
    return pl.pallas_call(
        functools.partial(_fox_decode_body, pages_per_step=npg),
        out_shape=jax.ShapeDtypeStruct((B, H, hd), F32),
        grid_spec=grid_spec,
        compiler_params=_params("parallel", "arbitrary"),
        name="fox_decode",
    )(page_table.reshape(-1), q, *([cache_k] * npg), *([cache_v] * npg), *([logf_t] * npg),
      k_new, v_new, lf_new, later)


def _first_argmax(x, n):
    io = lax.broadcasted_iota(jnp.int32, x.shape, 0)
    m = jnp.max(x, axis=0, keepdims=True)
    i = jnp.min(jnp.where(x == m, io, n), axis=0, keepdims=True)
    return m, i, io


def _stack_rows(rows, dtype):
    n = len(rows)
    io = lax.broadcasted_iota(jnp.int32, (n, rows[0].shape[1]), 0)
    out = jnp.broadcast_to(rows[0], io.shape).astype(dtype)
    for r in range(1, n):
        out = jnp.where(io == r, rows[r].astype(dtype), out)
    return out


def _route_body(h_ref, whi_ref, wlo_ref, b_ref, before_ref, cnt0_ref, idx_ref, w_ref, rank_ref, cnt_ref, carry_s):
    @pl.when(pl.program_id(0) == 0)
    def _init():
        carry_s[...] = cnt0_ref[...]

    h = h_ref[...]
    h_hi = h.astype(BF16)
    h_lo = (h - h_hi.astype(F32)).astype(BF16)
    whi = whi_ref[...]
    logits = _dot_nt(whi, h_hi) + _dot_nt(whi, h_lo) + _dot_nt(wlo_ref[...], h_hi)
    s = jax.nn.sigmoid(logits)
    sb = s + b_ref[...]
    E = sb.shape[0]
    gsz = E // N_GROUPS
    neg = -jnp.inf

    gscore = []
    for g in range(N_GROUPS):
        x = sb[g * gsz:(g + 1) * gsz]
        m1, i1, io = _first_argmax(x, gsz)
        m2 = jnp.max(jnp.where(io == i1, neg, x), axis=0, keepdims=True)
        gscore.append(m1 + m2)
    cur = _stack_rows(gscore, F32)
    keep = jnp.zeros(cur.shape, F32)
    for _ in range(TOPK_GROUPS):
        _, ig, iog = _first_argmax(cur, N_GROUPS)
        hit = iog == ig
        keep = jnp.where(hit, 1.0, keep)
        cur = jnp.where(hit, neg, cur)
    blocks = []
    for g in range(N_GROUPS):
        keep_g = jnp.max(jnp.where(iog == g, keep, 0.0), axis=0, keepdims=True)
        blocks.append(jnp.where(keep_g > 0.0, sb[g * gsz:(g + 1) * gsz], neg))
    cand = jnp.concatenate(blocks, axis=0)

    sel = jnp.zeros(cand.shape, F32)
    idxs, ws = [], []
    for _ in range(TOP_K):
        _, ik, ioe = _first_argmax(cand, E)
        hit = ioe == ik
        idxs.append(ik)
        ws.append(jnp.sum(jnp.where(hit, s, 0.0), axis=0, keepdims=True))
        cand = jnp.where(hit, neg, cand)
        sel = jnp.where(hit, 1.0, sel)
    wsum = ws[0]
    for k in range(1, TOP_K):
        wsum = wsum + ws[k]
    ws = [w / wsum * ROUTED_SCALE for w in ws]

    prefix = _dot(sel.astype(BF16), before_ref[...]) + carry_s[...]
    ranks = [jnp.sum(jnp.where(ioe == ik, prefix, 0.0), axis=0, keepdims=True) for ik in idxs]
    carry_s[...] = carry_s[...] + jnp.sum(sel, axis=1, keepdims=True)
    cnt_ref[...] = carry_s[...]
    idx_ref[...] = _stack_rows(idxs, jnp.int32)
    w_ref[...] = _stack_rows(ws, F32)
    rank_ref[...] = _stack_rows(ranks, F32).astype(jnp.int32)


def route_tokens(h, w_hi_t, w_lo_t, b_col, counts_in, tm=640):
    T, D = h.shape
    E = w_hi_t.shape[0]
    tm = _row_tile(T, tm)
    tok = lax.broadcasted_iota(jnp.int32, (tm, tm), 0)
    before = (tok < tok.T).astype(BF16)
    kt = lambda i: (0, i)
    const = lambda i: (0, 0)
    return pl.pallas_call(
        _route_body,
        out_shape=[jax.ShapeDtypeStruct((TOP_K, T), jnp.int32), jax.ShapeDtypeStruct((TOP_K, T), F32),
                   jax.ShapeDtypeStruct((TOP_K, T), jnp.int32), jax.ShapeDtypeStruct((E, 1), F32)],
        grid=(T // tm,),
        in_specs=[pl.BlockSpec((tm, D), lambda i: (i, 0)),
                  pl.BlockSpec((E, D), const), pl.BlockSpec((E, D), const),
                  pl.BlockSpec((E, 1), const), pl.BlockSpec((tm, tm), const), pl.BlockSpec((E, 1), const)],
        out_specs=[pl.BlockSpec((TOP_K, tm), kt), pl.BlockSpec((TOP_K, tm), kt),
                   pl.BlockSpec((TOP_K, tm), kt), pl.BlockSpec((E, 1), const)],
        scratch_shapes=[pltpu.VMEM((E, 1), F32)],
        compiler_params=_params("arbitrary"),
        name="route_tokens",
    )(h, w_hi_t, w_lo_t, b_col, before, counts_in)


def _slot(ps_ref, idx_ref, rank_ref, k, t):
    return ps_ref[idx_ref[k, t]] + rank_ref[k, t]


def _dispatch_body(ps_ref, idx_ref, rank_ref, h_ref, xs_in_ref, xs_ref, sem):
    del xs_in_ref
    tq = h_ref.shape[0]

    def row_copy(t, k):
        return pltpu.make_async_copy(h_ref.at[pl.ds(t, 1)],
                                     xs_ref.at[pl.ds(_slot(ps_ref, idx_ref, rank_ref, k, t), 1)], sem)

    def start(t, c):
        for k in range(TOP_K):
            row_copy(t, k).start(priority=k % 2)
        return c

    def wait(t, c):
        for k in range(TOP_K):
            row_copy(t, k).wait()
        return c

    lax.fori_loop(0, tq, start, 0)
    lax.fori_loop(0, tq, wait, 0)


def moe_dispatch(pad_start, idx, rank, h_packed, xs):
    T, Dp = h_packed.shape
    tq = MOE_DMA_TOKENS
    tq = _row_tile(T, tq)
    smem = lambda: pl.BlockSpec((TOP_K, tq), lambda i, ps: (0, i), memory_space=pltpu.SMEM)
    grid_spec = pltpu.PrefetchScalarGridSpec(
        num_scalar_prefetch=1,
        grid=(T // tq,),
        in_specs=[smem(), smem(),
                  pl.BlockSpec((tq, Dp), lambda i, ps: (i, 0)),
                  pl.BlockSpec(memory_space=pl.ANY)],
        out_specs=pl.BlockSpec(memory_space=pl.ANY),
        scratch_shapes=[pltpu.SemaphoreType.DMA(())])
    return pl.pallas_call(
        _dispatch_body,
        out_shape=jax.ShapeDtypeStruct(xs.shape, xs.dtype),
        grid_spec=grid_spec,
        input_output_aliases={4: 0},
        compiler_params=_params("arbitrary"),
        name="moe_dispatch",
    )(pad_start, idx, rank, h_packed, xs)


def _combine_body(ps_ref, idx_ref, rank_ref, w_ref, yb_ref, o_ref, buf, sem):
    tq = o_ref.shape[0]

    def row_copy(t, k):
        return pltpu.make_async_copy(yb_ref.at[pl.ds(_slot(ps_ref, idx_ref, rank_ref, k, t), 1)],
                                     buf.at[k, pl.ds(t, 1)], sem)

    def start(t, c):
        for k in range(TOP_K):
            row_copy(t, k).start(priority=k % 2)
        return c

    def wait(t, c):
        for k in range(TOP_K):
            row_copy(t, k).wait()
        return c

    lax.fori_loop(0, tq, start, 0)
    lax.fori_loop(0, tq, wait, 0)
    acc = w_ref[:, 0:1] * buf[0]
    for k in range(1, TOP_K):
        acc = acc + w_ref[:, k:k + 1] * buf[k]
    o_ref[...] = acc


def moe_combine(pad_start, idx, rank, w_col, yb):
    T = w_col.shape[0]
    D = yb.shape[1]
    tq = _row_tile(T, MOE_DMA_TOKENS)
    smem = lambda: pl.BlockSpec((TOP_K, tq), lambda i, ps: (0, i), memory_space=pltpu.SMEM)
    grid_spec = pltpu.PrefetchScalarGridSpec(
        num_scalar_prefetch=1,
        grid=(T // tq,),
        in_specs=[smem(), smem(),
                  pl.BlockSpec((tq, TOP_K), lambda i, ps: (i, 0)),
                  pl.BlockSpec(memory_space=pl.ANY)],
        out_specs=pl.BlockSpec((tq, D), lambda i, ps: (i, 0)),
        scratch_shapes=[pltpu.VMEM((TOP_K, tq, D), F32), pltpu.SemaphoreType.DMA(())])
    return pl.pallas_call(
        _combine_body,
        out_shape=jax.ShapeDtypeStruct((T, D), F32),
        grid_spec=grid_spec,
        compiler_params=_params("arbitrary"),
        name="moe_combine",
    )(pad_start, idx, rank, w_col, yb)


def _moe_body(seq_ref, tj_ref, tf_ref, na_ref, nv_ref, x_ref, wg_ref, wu_ref, wd_ref, y_ref,
              g_buf, u_buf, d_buf, wg_s, wu_s, wd_s, sem):
    i = pl.program_id(0)

    def fetch(j):
        e = seq_ref[j]
        slot = j % 2
        half = d_buf.shape[1] // 2
        parts = ((wg_ref.at[e], g_buf.at[slot]),
                 (wu_ref.at[e], u_buf.at[slot]),
                 (wd_ref.at[e, pl.ds(0, half)], d_buf.at[slot, pl.ds(0, half)]),
                 (wd_ref.at[e, pl.ds(half, half)], d_buf.at[slot, pl.ds(half, half)]))
        return [pltpu.make_async_copy(src, dst, sem.at[slot, n]) for n, (src, dst) in enumerate(parts)]

    def start_fetch(j):
        for n, c in enumerate(fetch(j)):
            c.start(priority=n % 2)

    @pl.when(i == 0)
    def _prime():
        start_fetch(0)

        @pl.when(na_ref[0] > 1)
        def _second():
            start_fetch(1)

    @pl.when(i >= nv_ref[0])
    def _unused_tile():
        y_ref[...] = jnp.zeros_like(y_ref)

    @pl.when(jnp.logical_and(i < nv_ref[0], tf_ref[i] == 1))
    def _load_expert():
        j = tj_ref[i]
        slot = j % 2
        for c in fetch(j):
            c.wait()
        wg_s[...] = g_buf[slot].astype(BF16)
        wu_s[...] = u_buf[slot].astype(BF16)
        wd_s[...] = d_buf[slot].astype(BF16)

        @pl.when(j + 2 < na_ref[0])
        def _refill():
            start_fetch(j + 2)

    @pl.when(i < nv_ref[0])
    def _tile():
        xw = x_ref[...]
        half = xw.shape[1]
        x_lo = lax.bitcast_convert_type(xw << 16, F32).astype(BF16)
        x_hi = lax.bitcast_convert_type(xw & jnp.int32(-65536), F32).astype(BF16)
        g = _dot(x_lo, wg_s[0:half]) + _dot(x_hi, wg_s[half:2 * half])
        u = _dot(x_lo, wu_s[0:half]) + _dot(x_hi, wu_s[half:2 * half])
        y_ref[...] = _dot((jax.nn.silu(g) * u).astype(BF16), wd_s[...])


def moe_experts(xs, expert_seq, tile_seq, tile_first, n_active, n_valid, w_gate, w_up, w_down):
    R, Dp = xs.shape
    D = 2 * Dp
    tm = MOE_TILE
    nt = R // tm
    De = w_gate.shape[2]
    grid_spec = pltpu.PrefetchScalarGridSpec(
        num_scalar_prefetch=5,
        grid=(nt,),
        in_specs=[pl.BlockSpec((tm, Dp), lambda i, *_: (i, 0)),
                  pl.BlockSpec(memory_space=pl.ANY),
                  pl.BlockSpec(memory_space=pl.ANY),
                  pl.BlockSpec(memory_space=pl.ANY)],
        out_specs=pl.BlockSpec((tm, D), lambda i, *_: (i, 0)),
        scratch_shapes=[pltpu.VMEM((2, D, De), F32), pltpu.VMEM((2, D, De), F32), pltpu.VMEM((2, De, D), F32),
                        pltpu.VMEM((D, De), BF16), pltpu.VMEM((D, De), BF16), pltpu.VMEM((De, D), BF16),
                        pltpu.SemaphoreType.DMA((2, 4))])
    return pl.pallas_call(
        _moe_body,
        out_shape=jax.ShapeDtypeStruct((R, D), F32),
        grid_spec=grid_spec,
        compiler_params=_params("arbitrary"),
        name="moe_experts",
    )(expert_seq, tile_seq, tile_first, n_active, n_valid, xs, w_gate, w_up, w_down)


def routed_moe(groups, w_r_hi_t, w_r_lo_t, b_router, w_gate, w_up, w_down):
    E = N_EXPERTS
    tm = MOE_TILE
    b_col = b_router.astype(F32).reshape(E, 1)
    cnt = jnp.zeros((E, 1), F32)
    routes = []
    for h, _ in groups:
        idx, w, rank, cnt = route_tokens(h, w_r_hi_t, w_r_lo_t, b_col, cnt)
        routes.append((idx, w, rank))

    counts = cnt[:, 0].astype(jnp.int32)
    padded = (counts + tm - 1) // tm * tm
    pad_end = jnp.cumsum(padded)
    pad_start = pad_end - padded
    n_tokens = sum(h.shape[0] for h, _ in groups)
    nt = -(-(n_tokens * TOP_K) // tm) + E
    n_valid = (pad_end[-1] // tm).astype(jnp.int32)
    tile_row = jnp.minimum(jnp.arange(nt, dtype=jnp.int32), n_valid - 1)
    tile_expert = jnp.minimum(jnp.sum((pad_end[None, :] <= (tile_row * tm)[:, None]).astype(jnp.int32), axis=1),
                              E - 1)
    tile_first = jnp.concatenate([jnp.ones((1,), jnp.int32),
                                  (tile_expert[1:] != tile_expert[:-1]).astype(jnp.int32)])
    seq_of = jnp.cumsum((counts > 0).astype(jnp.int32)) - 1
    n_active = seq_of[-1] + 1
    expert_seq = jnp.minimum(jnp.sum((seq_of[None, :] < jnp.arange(E, dtype=jnp.int32)[:, None]).astype(jnp.int32),
                                     axis=1), E - 1)
    tile_seq = seq_of[tile_expert]

    xs = jnp.zeros((nt * tm, groups[0][1].shape[1]), jnp.int32)
    for (_, h_packed), (idx, _, rank) in zip(groups, routes):
        xs = moe_dispatch(pad_start, idx, rank, h_packed, xs)
    yb = moe_experts(xs, expert_seq, tile_seq, tile_first, n_active.reshape(1), n_valid.reshape(1),
                     w_gate, w_up, w_down)
    return [moe_combine(pad_start, idx, rank, w.T, yb) for idx, w, rank in routes]


def _prep_weights(w_in, b_ml_i, b_ml_f, b_fx_f, w_br_ml, w_br_fx, w_out, w_router,
                  w_sh_gate, w_sh_up, w_sh_down, w_ple_gate, w_ple_proj):
    o_ml = 2 * ML_W_QK + 2 * ML_W_V
    o_fx = o_ml + 2 * ML_HEADS
    o_ff = o_fx + 3 * FX_W
    o_g = o_ff + FX_HEADS
    wb = w_in.astype(BF16)
    small = jnp.concatenate([wb[:, o_ml:o_fx], wb[:, o_ff:o_g]], axis=1)
    small = jnp.pad(small, ((0, 0), (0, LANES - small.shape[1])))
    w_r_hi = _round_bf16(w_router.T)
    w_r_lo = (w_router.T - w_r_hi).astype(BF16)
    w_r_hi = w_r_hi.astype(BF16)
    return dict(
        w_ml=wb[:, :o_ml], w_fq=wb[:, o_fx:o_fx + FX_W], w_fk=wb[:, o_fx + FX_W:o_fx + 2 * FX_W],
        w_fv=wb[:, o_fx + 2 * FX_W:o_ff], w_g=wb[:, o_g:], w_small=small,
        w_br_ml=w_br_ml.astype(BF16), w_br_fx=w_br_fx.astype(BF16), w_out=w_out.astype(BF16),
        w_r_hi=w_r_hi, w_r_lo=w_r_lo,
        w_sh_gate=w_sh_gate.astype(BF16), w_sh_up=w_sh_up.astype(BF16), w_sh_down=w_sh_down.astype(BF16),
        w_ple_gate=w_ple_gate.astype(BF16), w_ple_proj=w_ple_proj.astype(BF16))


def _in_proj(x_bf, W):
    (z_ml,) = matmul(x_bf, W['w_ml'], name="proj_ml")
    (fq,) = matmul(x_bf, W['w_fq'], out_dtypes=(BF16,), scale=FX_SCALE, name="proj_fq")
    fk, fk_bf = matmul(x_bf, W['w_fk'], out_dtypes=(F32, BF16), name="proj_fk")
    fv, fv_bf = matmul(x_bf, W['w_fv'], out_dtypes=(F32, BF16), name="proj_fv")
    (z_g,) = matmul(x_bf, W['w_g'], name="proj_gates")
    (z_s,) = matmul(x_bf, W['w_small'], name="proj_small")
    return z_ml, fq, fk, fk_bf, fv, fv_bf, z_g, z_s


def _small_gates(z_s, b_ml_i, b_ml_f, b_fx_f):
    H = ML_HEADS
    li = z_s[:, 0:H] + b_ml_i
    lf = jax.nn.log_sigmoid(z_s[:, H:2 * H] + b_ml_f)
    lf_fx = jax.nn.log_sigmoid(z_s[:, 2 * H:2 * H + FX_HEADS] + b_fx_f)
    return li, lf, lf_fx


def kernel(x_prompt, x_sample, p_prompt, p_sample, cache_k, cache_v, cache_logf, state_C, state_n, state_m,
           page_table, w_in, b_ml_i, b_ml_f, b_fx_f, ml_norm_g, w_br_ml, w_br_fx, w_out, ln1_g, ln1_b,
           w_router, b_router, w_exp_gate, w_exp_up, w_exp_down, w_sh_gate, w_sh_up, w_sh_down,
           w_ple_gate, w_ple_proj, ln2_g, ln2_b):
    T = x_prompt.shape[1]
    B = x_sample.shape[0]
    H = ML_HEADS
    W = _prep_weights(w_in[0], b_ml_i[0], b_ml_f[0], b_fx_f[0], w_br_ml[0], w_br_fx[0], w_out[0], w_router[0],
                      w_sh_gate[0], w_sh_up[0], w_sh_down[0], w_ple_gate[0], w_ple_proj[0])
    xp = x_prompt[0]
    xs = x_sample[:, 0]

    z_ml, fq, fk, fk_bf, fv, fv_bf, z_g, z_s = _in_proj(xp.astype(BF16), W)
    li, lf, lf_fx = _small_gates(z_s, b_ml_i[0], b_ml_f[0], b_fx_f[0])
    nc = T // ML_CHUNK
    to_rows = lambda a: a.T.reshape(H, nc, 1, ML_CHUNK)
    hm_p, C_p, n_p, m_p = mlstm_prompt(z_ml, to_rows(li), to_rows(lf), ml_norm_g)
    hfx_p = fox_prompt(fq, fk_bf, fv_bf, -jnp.cumsum(lf_fx, axis=0))
    merged_p = gated_merge(hm_p, W['w_br_ml'], hfx_p, W['w_br_fx'], z_g, BF16)
    h1_p, h1b_p, h1k_p = out_proj_ln(merged_p, W['w_out'], xp, ln1_g, ln1_b)

    zd_ml, dq, dk, dk_bf, dv, dv_bf, zd_g, zd_s = _in_proj(xs.astype(BF16), W)
    dli, dlf, dlf_fx = _small_gates(zd_s, b_ml_i[0], b_ml_f[0], b_fx_f[0])
    col = lambda a: a.reshape(B, H, ML_DQK, 1)
    gates = jnp.stack([dli, dlf, state_m[0]], axis=-1)
    gates = jnp.pad(gates, ((0, 0), (0, 0), (0, LANES - 3))).reshape(B, H, 1, LANES)
    hm_d, C_d, n_d, m_d = mlstm_decode(
        state_C[0], col(state_n[0]), col(zd_ml[:, :ML_W_QK]), col(zd_ml[:, ML_W_QK:2 * ML_W_QK]),
        zd_ml[:, 2 * ML_W_QK:2 * ML_W_QK + ML_W_V].reshape(B, H, 1, ML_DV),
        zd_ml[:, 2 * ML_W_QK + ML_W_V:].reshape(B, H, 1, ML_DV),
        gates, ml_norm_g.reshape(H, 1, ML_DV))
    n_pool = cache_k.shape[1]
    page_rows = PAGE_SIZE * FX_HEADS
    row_of = jnp.arange(page_rows, dtype=jnp.int32) // FX_HEADS
    later = (jnp.arange(PAGE_SIZE, dtype=jnp.int32)[None, :] > row_of[:, None]).astype(BF16)
    heads = lambda a: a.reshape(B, FX_HEADS, FX_HD)
    hfx_d = fox_decode(page_table, heads(dq), cache_k.reshape(n_pool, page_rows, FX_HD),
                       cache_v.reshape(n_pool, page_rows, FX_HD), jnp.swapaxes(cache_logf[0], 1, 2),
                       heads(dk), heads(dv), dlf_fx[:, :, None], later)
    merged_d = gated_merge(hm_d.reshape(B, ML_W_V), W['w_br_ml'], hfx_d.reshape(B, FX_W).astype(BF16),
                           W['w_br_fx'], zd_g, BF16)
    h1_d, h1b_d, h1k_d = out_proj_ln(merged_d, W['w_out'], xs, ln1_g, ln1_b)

    moe_p, moe_d = routed_moe([(h1_p, h1k_p), (h1_d, h1k_d)], W['w_r_hi'], W['w_r_lo'], b_router[0],
                              w_exp_gate[0], w_exp_up[0], w_exp_down[0])
    ple_p = ple_embed(h1b_p, W['w_ple_gate'], p_prompt[0, 0].astype(BF16), W['w_ple_proj'])
    ple_d = ple_embed(h1b_d, W['w_ple_gate'], p_sample[0, :, 0].astype(BF16), W['w_ple_proj'])
    y_p = ffn_tail(h1b_p, h1_p, moe_p, ple_p, W['w_sh_gate'], W['w_sh_up'], W['w_sh_down'], ln2_g, ln2_b)
    y_d = ffn_tail(h1b_d, h1_d, moe_d, ple_d, W['w_sh_gate'], W['w_sh_up'], W['w_sh_down'], ln2_g, ln2_b)

    return (y_p[None], y_d[:, None],
            fk.reshape(1, 1, T, FX_HEADS, FX_HD), fv.reshape(1, 1, T, FX_HEADS, FX_HD), lf_fx[None, None],
            C_p[None, None], n_p.reshape(1, 1, H, ML_DQK), m_p[:, 0, 0].reshape(1, 1, H),
            dk.reshape(1, B, 1, FX_HEADS, FX_HD), dv.reshape(1, B, 1, FX_HEADS, FX_HD), dlf_fx[None, :, None],
            C_d[None], n_d.reshape(1, B, H, ML_DQK), m_d[:, :, 0, 0][None])
```
